```python
import jax, jax.numpy as jnp
from jax import lax
import numpy as np

D_MODEL = 1024
BATCH = 32
SEQ = 2048
DEPTH = 4

N_MIXERS = 2
SSD_EXPAND = 2
D_INNER = SSD_EXPAND * D_MODEL
HEAD_DIM = 64
N_SSD_HEADS = D_INNER // HEAD_DIM
N_SSD_GROUPS = 4
HEADS_PER_GROUP = N_SSD_HEADS // N_SSD_GROUPS
D_STATE = 128
SSD_CONV = 4
CHUNK = 128
D_XBC = D_INNER + 2 * N_SSD_GROUPS * D_STATE
D_IN_PROJ = D_INNER + D_XBC + N_SSD_HEADS
DT_MIN = 1e-3
DT_MAX = 1e-1
A_MIN = 1.0
A_MAX = 16.0
POOL_WINDOWS = (2, 4, 8, 16)
N_POOL_GROUPS = len(POOL_WINDOWS)
POOL_GROUP_DIM = D_MODEL // N_POOL_GROUPS
D_FF = 2816
FFN_CONV = 3
EPS = 1e-6

kernel_name = 'hybrid_ssd_pool_convffn_trunk'


def rms_norm(x, w):
    xf = x.astype(jnp.float32)
    xf = xf * lax.rsqrt(jnp.mean(xf * xf, axis=-1, keepdims=True) + EPS)
    return (xf * w.astype(jnp.float32)).astype(x.dtype)


def causal_depthwise_conv(x, w, b):
    k = w.shape[0]
    y = lax.conv_general_dilated(x, w[:, None, :].astype(x.dtype), window_strides=(1,),
                                 padding=[(k - 1, 0)],
                                 dimension_numbers=('NWC', 'WIO', 'NWC'),
                                 feature_group_count=x.shape[-1])
    return y + b.astype(x.dtype)


def segsum_decay(a_cum):
    q = a_cum.shape[-1]
    mask = jnp.tril(jnp.ones((q, q), dtype=bool))
    diff = a_cum[..., :, None] - a_cum[..., None, :]
    return jnp.exp(jnp.where(mask, diff, -jnp.inf))


def ssd_chunked(xh, dt, a, bm, cm):
    b, l, h, p = xh.shape
    g, n = bm.shape[2], bm.shape[3]
    r = h // g
    nc = l // CHUNK
    xf = xh.astype(jnp.float32).reshape(b, nc, CHUNK, g, r, p)
    dtc = dt.reshape(b, nc, CHUNK, g, r)
    bc = bm.astype(jnp.float32).reshape(b, nc, CHUNK, g, n)
    cc = cm.astype(jnp.float32).reshape(b, nc, CHUNK, g, n)
    a_dt = jnp.moveaxis(dtc * a.reshape(g, r), 2, -1)
    a_cum = jnp.cumsum(a_dt, axis=-1)
    xdt = xf * dtc[..., None]
    decay = segsum_decay(a_cum)
    cb = jnp.einsum('bclgn,bcsgn->bcgls', cc, bc)
    y_diag = jnp.einsum('bcgls,bcgrls,bcsgrp->bclgrp', cb, decay, xdt)
    decay_to_end = jnp.exp(a_cum[..., -1:] - a_cum)
    states = jnp.einsum('bcsgn,bcgrs,bcsgrp->bcgrpn', bc, decay_to_end, xdt)
    chunk_decay = jnp.exp(a_cum[..., -1])

    def step(carry, inp):
        st, dec = inp
        return carry * dec[..., None, None] + st, carry

    h0 = jnp.zeros((b, g, r, p, n), jnp.float32)
    _, prev = lax.scan(step, h0, (jnp.moveaxis(states, 1, 0), jnp.moveaxis(chunk_decay, 1, 0)))
    prev = jnp.moveaxis(prev, 0, 1)
    y_off = jnp.einsum('bclgn,bcgrpn,bcgrl->bclgrp', cc, prev, jnp.exp(a_cum))
    return (y_diag + y_off).reshape(b, l, h, p)


def ssd_mixer(u, w_in, conv_w, conv_b, dt_bias, a_log, d_skip, norm_w, w_out):
    b, l, _ = u.shape
    zxbcdt = u @ w_in.astype(u.dtype)
    z, xbc, dt = jnp.split(zxbcdt, [D_INNER, D_INNER + D_XBC], axis=-1)
    xbc = jax.nn.silu(causal_depthwise_conv(xbc, conv_w, conv_b))
    xs, bm, cm = jnp.split(xbc, [D_INNER, D_INNER + N_SSD_GROUPS * D_STATE], axis=-1)
    xh = xs.reshape(b, l, N_SSD_HEADS, HEAD_DIM)
    bm = bm.reshape(b, l, N_SSD_GROUPS, D_STATE)
    cm = cm.reshape(b, l, N_SSD_GROUPS, D_STATE)
    dt = jax.nn.softplus(dt.astype(jnp.float32) + dt_bias.astype(jnp.float32))
    a = -jnp.exp(a_log.astype(jnp.float32))
    y = ssd_chunked(xh, dt, a, bm, cm)
    y = y + d_skip.astype(jnp.float32)[:, None] * xh.astype(jnp.float32)
    y = y.reshape(b, l, D_INNER) * jax.nn.silu(z.astype(jnp.float32))
    yg = y.reshape(b, l, N_SSD_GROUPS, D_INNER // N_SSD_GROUPS)
    yg = yg * lax.rsqrt(jnp.mean(yg * yg, axis=-1, keepdims=True) + EPS)
    y = (yg.reshape(b, l, D_INNER) * norm_w.astype(jnp.float32)).astype(u.dtype)
    return y @ w_out.astype(u.dtype)


def pool_mixer(u, w_pool, scale):
    b, l, d = u.shape
    uf = u.astype(jnp.float32)
    cs = jnp.pad(jnp.cumsum(uf, axis=1), ((0, 0), (1, 0), (0, 0)))
    pos = jnp.arange(1, l + 1, dtype=jnp.float32)[None, :, None]
    groups = []
    for k, w in enumerate(POOL_WINDOWS):
        cg = cs[..., k * POOL_GROUP_DIM:(k + 1) * POOL_GROUP_DIM]
        lagged = jnp.pad(cg, ((0, 0), (w - 1, 0), (0, 0)))[:, :l]
        mean = (cg[:, 1:] - lagged) / jnp.minimum(pos, float(w))
        groups.append(mean - uf[..., k * POOL_GROUP_DIM:(k + 1) * POOL_GROUP_DIM])
    mixed = jnp.stack(groups, axis=2)
    out = jnp.einsum('blgc,gcd->blgd', mixed, w_pool.astype(jnp.float32)).reshape(b, l, d)
    return (out * scale.astype(jnp.float32)).astype(u.dtype)


def conv_ffn(u, w_up, conv_w, conv_b, w_down):
    h = u @ w_up.astype(u.dtype)
    h = causal_depthwise_conv(h, conv_w, conv_b)
    gate, val = jnp.split(h, 2, axis=-1)
    return (jax.nn.silu(gate) * val) @ w_down.astype(u.dtype)


def setup_inputs(seed: int = 0) -> dict:
    key = jax.random.key(seed)
    ks = jax.random.split(key, 24)
    n_ssd = (DEPTH + N_MIXERS - 1) // N_MIXERS
    n_pool = DEPTH // N_MIXERS
    nrm = jax.random.normal
    x = nrm(ks[0], (BATCH, SEQ, D_MODEL), jnp.float32)
    ssd_w_in = nrm(ks[1], (n_ssd, D_MODEL, D_IN_PROJ), jnp.float32) * D_MODEL ** -0.5
    ssd_conv_w = nrm(ks[2], (n_ssd, SSD_CONV, D_XBC), jnp.float32) * SSD_CONV ** -0.5
    ssd_conv_b = nrm(ks[3], (n_ssd, D_XBC), jnp.float32) * 0.01
    u = jax.random.uniform(ks[4], (n_ssd, N_SSD_HEADS), jnp.float32)
    dt0 = jnp.exp(u * (np.log(DT_MAX) - np.log(DT_MIN)) + np.log(DT_MIN))
    ssd_dt_bias = dt0 + jnp.log(-jnp.expm1(-dt0))
    ssd_a_log = jnp.log(jax.random.uniform(ks[5], (n_ssd, N_SSD_HEADS), jnp.float32, A_MIN, A_MAX))
    ssd_d = 1.0 + 0.1 * nrm(ks[6], (n_ssd, N_SSD_HEADS), jnp.float32)
    ssd_norm_w = 1.0 + 0.1 * nrm(ks[7], (n_ssd, D_INNER), jnp.float32)
    ssd_w_out = nrm(ks[8], (n_ssd, D_INNER, D_MODEL), jnp.float32) * D_INNER ** -0.5
    pool_w = nrm(ks[9], (n_pool, N_POOL_GROUPS, POOL_GROUP_DIM, POOL_GROUP_DIM), jnp.float32) * POOL_GROUP_DIM ** -0.5
    pool_scale = 1.0 + 0.1 * nrm(ks[10], (n_pool, D_MODEL), jnp.float32)
    ffn_w_up = nrm(ks[11], (DEPTH, D_MODEL, 2 * D_FF), jnp.float32) * D_MODEL ** -0.5
    ffn_conv_w = nrm(ks[12], (DEPTH, FFN_CONV, 2 * D_FF), jnp.float32) * FFN_CONV ** -0.5
    ffn_conv_b = nrm(ks[13], (DEPTH, 2 * D_FF), jnp.float32) * 0.01
    ffn_w_down = nrm(ks[14], (DEPTH, D_FF, D_MODEL), jnp.float32) * D_FF ** -0.5
    norm_mix_pre = 1.0 + 0.1 * nrm(ks[15], (DEPTH, D_MODEL), jnp.float32)
    norm_mix_post = 1.0 + 0.1 * nrm(ks[16], (DEPTH, D_MODEL), jnp.float32)
    norm_ffn_pre = 1.0 + 0.1 * nrm(ks[17], (DEPTH, D_MODEL), jnp.float32)
    norm_ffn_post = 1.0 + 0.1 * nrm(ks[18], (DEPTH, D_MODEL), jnp.float32)
    return {'x': x, 'ssd_w_in': ssd_w_in, 'ssd_conv_w': ssd_conv_w, 'ssd_conv_b': ssd_conv_b,
            'ssd_dt_bias': ssd_dt_bias, 'ssd_a_log': ssd_a_log, 'ssd_d': ssd_d,
            'ssd_norm_w': ssd_norm_w, 'ssd_w_out': ssd_w_out, 'pool_w': pool_w,
            'pool_scale': pool_scale, 'ffn_w_up': ffn_w_up, 'ffn_conv_w': ffn_conv_w,
            'ffn_conv_b': ffn_conv_b, 'ffn_w_down': ffn_w_down, 'norm_mix_pre': norm_mix_pre,
            'norm_mix_post': norm_mix_post, 'norm_ffn_pre': norm_ffn_pre,
            'norm_ffn_post': norm_ffn_post}


def reference(x, ssd_w_in, ssd_conv_w, ssd_conv_b, ssd_dt_bias, ssd_a_log, ssd_d,
              ssd_norm_w, ssd_w_out, pool_w, pool_scale, ffn_w_up, ffn_conv_w, ffn_conv_b,
              ffn_w_down, norm_mix_pre, norm_mix_post, norm_ffn_pre, norm_ffn_post):
    for i in range(DEPTH):
        j = i // N_MIXERS
        h = rms_norm(x, norm_mix_pre[i])
        if i % N_MIXERS == 0:
            mix = ssd_mixer(h, ssd_w_in[j], ssd_conv_w[j], ssd_conv_b[j], ssd_dt_bias[j],
                            ssd_a_log[j], ssd_d[j], ssd_norm_w[j], ssd_w_out[j])
        else:
            mix = pool_mixer(h, pool_w[j], pool_scale[j])
        x = x + rms_norm(mix, norm_mix_post[i])
        f = conv_ffn(rms_norm(x, norm_ffn_pre[i]), ffn_w_up[i], ffn_conv_w[i], ffn_conv_b[i],
                     ffn_w_down[i])
        x = x + rms_norm(f, norm_ffn_post[i])
    return x
```

```python
import functools

import jax
import jax.numpy as jnp
from jax import lax
from jax.experimental import pallas as pl
from jax.experimental.pallas import tpu as pltpu

F32 = jnp.float32
BF16 = jnp.bfloat16

EPS = 1e-6
HEAD_DIM = 64
D_STATE = 128
N_SSD_GROUPS = 4
CHUNK = 128
POOL_WINDOWS = (2, 4, 8, 16)
LANES = 128
CARRY_ROWS = 8
POOL_CARRY_ROWS = 16
VMEM_LIMIT_BYTES = 56 * 1024 * 1024

FFN_TILE = 256
POOL_TILE = 256
SSD_TILE = 256
FFN_COL_BLOCK = 256
PROJ_COL_BLOCK = 512
CONV_COL_BLOCK = 256


def _rms(x, w):
    ms = jnp.mean(x * x, axis=-1, keepdims=True)
    return x * lax.rsqrt(ms + EPS) * w


def _sigmoid(x):
    return 1.0 / (1.0 + jnp.exp(-x))


def _shift_rows(u, prev, k, row):
    r = pltpu.roll(u, k, axis=0)
    n = prev.shape[0]
    for j in range(k):
        r = jnp.where(row == j, prev[n - k + j:n - k + j + 1, :], r)
    return r


def _split_bf16(x, parts):
    out = []
    rem = x
    for _ in range(parts):
        p = rem.astype(BF16)
        out.append(p)
        rem = rem - p.astype(F32)
    return out


def _const_spec(shape):
    nd = len(shape)
    return pl.BlockSpec(shape, lambda b, t: (0,) * nd)


def _row_spec(tm, d, nt):
    return pl.BlockSpec((tm, d), lambda b, t: (b * nt + t, 0))


def _params():
    return pltpu.CompilerParams(dimension_semantics=("arbitrary", "arbitrary"),
                                vmem_limit_bytes=VMEM_LIMIT_BYTES)


def _ffn_body(x_ref, wpre_ref, wup_ref, cw_ref, cb_ref, wdn_ref, wpost_ref, o_ref,
              carry_ref, act_ref, *, tm, dff, fc):
    @pl.when(pl.program_id(1) == 0)
    def _():
        carry_ref[...] = jnp.zeros_like(carry_ref)

    x = x_ref[...]
    h = _rms(x, wpre_ref[...]).astype(BF16)
    row = lax.broadcasted_iota(jnp.int32, (tm, 1), 0)

    def conv_half(col):
        u = jnp.dot(h, wup_ref[:, col:col + fc], preferred_element_type=F32)
        prev = carry_ref[:, col:col + fc]
        carry_ref[:, col:col + fc] = u[tm - CARRY_ROWS:, :]
        cw = cw_ref[:, col:col + fc]
        return (cb_ref[:, col:col + fc] + cw[2:3, :] * u
                + cw[1:2, :] * _shift_rows(u, prev, 1, row)
                + cw[0:1, :] * _shift_rows(u, prev, 2, row))

    for c in range(dff // fc):
        g = conv_half(c * fc)
        v = conv_half(dff + c * fc)
        act_ref[:, c * fc:(c + 1) * fc] = (g * _sigmoid(g) * v).astype(BF16)

    f = jnp.dot(act_ref[...], wdn_ref[...], preferred_element_type=F32)
    o_ref[...] = x + _rms(f, wpost_ref[...])


def _ffn_layer(x2, batch, wpre, wup, cw, cb, wdn, wpost):
    m, d = x2.shape
    seq = m // batch
    tm = min(FFN_TILE, seq)
    nt = seq // tm
    dff = wdn.shape[0]
    body = functools.partial(_ffn_body, tm=tm, dff=dff, fc=FFN_COL_BLOCK)
    return pl.pallas_call(
        body,
        grid=(batch, nt),
        in_specs=[_row_spec(tm, d, nt), _const_spec((1, d)), _const_spec(wup.shape),
                  _const_spec(cw.shape), _const_spec(cb.shape), _const_spec(wdn.shape),
                  _const_spec((1, d))],
        out_specs=_row_spec(tm, d, nt),
        out_shape=jax.ShapeDtypeStruct((m, d), F32),
        scratch_shapes=[pltpu.VMEM((CARRY_ROWS, 2 * dff), F32), pltpu.VMEM((tm, dff), BF16)],
        compiler_params=_params(),
        name="conv_ffn",
    )(x2, wpre, wup, cw, cb, wdn, wpost)


def _pool_body(x_ref, wpre_ref, wpool_ref, scale_ref, wpost_ref, o_ref, carry_ref, *, tm, dg):
    t = pl.program_id(1)

    @pl.when(t == 0)
    def _():
        carry_ref[...] = jnp.zeros_like(carry_ref)

    x = x_ref[...]
    h = _rms(x, wpre_ref[...])
    prev = carry_ref[...]
    carry_ref[...] = h[tm - POOL_CARRY_ROWS:, :]
    pos1 = (t * tm + 1 + lax.broadcasted_iota(jnp.int32, (tm, 1), 0)).astype(F32)

    outs = []
    for k, w in enumerate(POOL_WINDOWS):
        hg = h[:, k * dg:(k + 1) * dg]
        s = jnp.concatenate([prev[:, k * dg:(k + 1) * dg], hg], axis=0)
        step = 1
        while step < w:
            s = s + pltpu.roll(s, step, axis=0)
            step *= 2
        mean = s[POOL_CARRY_ROWS:, :] / jnp.minimum(pos1, float(w))
        mixed = (mean - hg).astype(BF16)
        outs.append(jnp.dot(mixed, wpool_ref[k], preferred_element_type=F32))
    out = jnp.concatenate(outs, axis=1) * scale_ref[...]
    o_ref[...] = x + _rms(out, wpost_ref[...])


def _pool_layer(x2, batch, wpre, wpool, scale, wpost):
    m, d = x2.shape
    seq = m // batch
    tm = min(POOL_TILE, seq)
    nt = seq // tm
    dg = wpool.shape[1]
    body = functools.partial(_pool_body, tm=tm, dg=dg)
    return pl.pallas_call(
        body,
        grid=(batch, nt),
        in_specs=[_row_spec(tm, d, nt), _const_spec((1, d)), _const_spec(wpool.shape),
                  _const_spec((1, d)), _const_spec((1, d))],
        out_specs=_row_spec(tm, d, nt),
        out_shape=jax.ShapeDtypeStruct((m, d), F32),
        scratch_shapes=[pltpu.VMEM((POOL_CARRY_ROWS, d), F32)],
        compiler_params=_params(),
        name="pool_mixer",
    )(x2, wpre, wpool, scale, wpost)


def _ssd_body(x_ref, wpre_ref, win_ref, cw_ref, cb_ref, dtb_ref, alog_ref, dexp_ref, nw_ref,
              wout_ref, wpost_ref, o_ref, proj_ref, carry_ref, state_ref, y_ref,
              *, tm, d_inner):
    q = CHUNK
    n = D_STATE
    gn = N_SSD_GROUPS * n
    gw = d_inner // N_SSD_GROUPS
    d_xbc = d_inner + 2 * gn
    off_x = d_inner
    off_b = off_x + d_inner
    off_c = off_b + gn
    off_dt = off_c + gn
    d_proj = off_dt + LANES

    @pl.when(pl.program_id(1) == 0)
    def _():
        carry_ref[...] = jnp.zeros_like(carry_ref)
        state_ref[...] = jnp.zeros_like(state_ref)

    x = x_ref[...]
    h = _rms(x, wpre_ref[...]).astype(BF16)
    for c0 in range(0, d_proj, PROJ_COL_BLOCK):
        c1 = min(c0 + PROJ_COL_BLOCK, d_proj)
        proj_ref[:, c0:c1] = jnp.dot(h, win_ref[:, c0:c1], preferred_element_type=F32)

    row = lax.broadcasted_iota(jnp.int32, (tm, 1), 0)
    for c0 in range(0, d_xbc, CONV_COL_BLOCK):
        c1 = c0 + CONV_COL_BLOCK
        u = proj_ref[:, off_x + c0:off_x + c1]
        prev = carry_ref[:, c0:c1]
        carry_ref[:, c0:c1] = u[tm - CARRY_ROWS:, :]
        cw = cw_ref[:, c0:c1]
        y = (cb_ref[:, c0:c1] + cw[3:4, :] * u
             + cw[2:3, :] * _shift_rows(u, prev, 1, row)
             + cw[1:2, :] * _shift_rows(u, prev, 2, row)
             + cw[0:1, :] * _shift_rows(u, prev, 3, row))
        proj_ref[:, off_x + c0:off_x + c1] = y * _sigmoid(y)

    a_neg = -jnp.exp(alog_ref[...])
    ri = lax.broadcasted_iota(jnp.int32, (q, q), 0)
    ci = lax.broadcasted_iota(jnp.int32, (q, q), 1)
    causal = ri >= ci
    tril = jnp.where(causal, 1.0, 0.0).astype(BF16)
    eh = lax.broadcasted_iota(jnp.int32, (LANES, d_inner), 0)
    ec = lax.broadcasted_iota(jnp.int32, (LANES, d_inner), 1)
    expand = jnp.where((ec >= eh * HEAD_DIM) & (ec < (eh + 1) * HEAD_DIM), 1.0, 0.0).astype(BF16)
    lane = lax.broadcasted_iota(jnp.int32, (q, LANES), 1)
    left = lane < HEAD_DIM

    for ch in range(tm // q):
        r0, r1 = ch * q, (ch + 1) * q
        v = proj_ref[r0:r1, off_dt:off_dt + LANES] + dtb_ref[...]
        dt = jnp.maximum(v, 0.0) + jnp.log1p(jnp.exp(-jnp.abs(v)))
        a_dt = dt * a_neg
        a_cum = sum(jnp.dot(tril, p, preferred_element_type=F32) for p in _split_bf16(a_dt, 3))
        a_cum_t = a_cum.T
        e_cum = jnp.exp(a_cum)
        d_end = jnp.exp(a_cum[q - 1:q, :] - a_cum)
        stack = jnp.concatenate([dt, e_cum, d_end], axis=0)
        wide = sum(jnp.dot(p, expand, preferred_element_type=F32) for p in _split_bf16(stack, 2))
        dt_x, e_cum_x, d_end_x = wide[0:q], wide[q:2 * q], wide[2 * q:3 * q]

        xs = proj_ref[r0:r1, off_x:off_x + d_inner]
        xdt = xs * dt_x
        xdte = (xdt * d_end_x).astype(BF16)

        for g in range(N_SSD_GROUPS):
            bg = proj_ref[r0:r1, off_b + g * n:off_b + (g + 1) * n]
            cg = proj_ref[r0:r1, off_c + g * n:off_c + (g + 1) * n].astype(BF16)
            cb = lax.dot_general(cg, bg.astype(BF16), (((1,), (1,)), ((), ())),
                                 preferred_element_type=F32)
            s_old = state_ref[g]
            y_off = (jnp.dot(cg, s_old.astype(BF16), preferred_element_type=F32)
                     * e_cum_x[:, g * gw:(g + 1) * gw])
            st = jnp.dot(bg.T.astype(BF16), xdte[:, g * gw:(g + 1) * gw],
                         preferred_element_type=F32)
            state_ref[g] = s_old * e_cum_x[q - 1:q, g * gw:(g + 1) * gw] + st

            pieces = []
            for j in range(gw // LANES):
                c0 = g * gw + j * LANES
                ms = []
                for hh in range(2):
                    head = c0 // HEAD_DIM + hh
                    col = jnp.sum(jnp.where(lane == head, a_cum, 0.0), axis=1, keepdims=True)
                    seg = col - a_cum_t[head:head + 1, :]
                    decay = jnp.exp(jnp.where(causal, seg, -jnp.inf))
                    ms.append((cb * decay).astype(BF16))
                xp = xdt[:, c0:c0 + LANES]
                rhs = jnp.concatenate([jnp.where(left, xp, 0.0), jnp.where(left, 0.0, xp)],
                                      axis=0).astype(BF16)
                y = jnp.dot(jnp.concatenate(ms, axis=1), rhs, preferred_element_type=F32)
                y = (y + y_off[:, j * LANES:(j + 1) * LANES]
                     + dexp_ref[:, c0:c0 + LANES] * xs[:, c0:c0 + LANES])
                z = proj_ref[r0:r1, c0:c0 + LANES]
                pieces.append(y * (z * _sigmoid(z)))
            yg = jnp.concatenate(pieces, axis=1)
            yg = yg * lax.rsqrt(jnp.mean(yg * yg, axis=-1, keepdims=True) + EPS)
            y_ref[r0:r1, g * gw:(g + 1) * gw] = (yg * nw_ref[:, g * gw:(g + 1) * gw]).astype(BF16)

    out = jnp.dot(y_ref[...], wout_ref[...], preferred_element_type=F32)
    o_ref[...] = x + _rms(out, wpost_ref[...])


def _ssd_layer(x2, batch, wpre, win, cw, cb, dtb, alog, dexp, nw, wout, wpost):
    m, d = x2.shape
    seq = m // batch
    tm = min(SSD_TILE, seq)
    nt = seq // tm
    d_inner = wout.shape[0]
    d_xbc = cw.shape[1]
    body = functools.partial(_ssd_body, tm=tm, d_inner=d_inner)
    return pl.pallas_call(
        body,
        grid=(batch, nt),
        in_specs=[_row_spec(tm, d, nt), _const_spec((1, d)), _const_spec(win.shape),
                  _const_spec(cw.shape), _const_spec(cb.shape), _const_spec(dtb.shape),
                  _const_spec(alog.shape), _const_spec(dexp.shape), _const_spec(nw.shape),
                  _const_spec(wout.shape), _const_spec((1, d))],
        out_specs=_row_spec(tm, d, nt),
        out_shape=jax.ShapeDtypeStruct((m, d), F32),
        scratch_shapes=[pltpu.VMEM((tm, win.shape[1]), F32),
                        pltpu.VMEM((CARRY_ROWS, d_xbc), F32),
                        pltpu.VMEM((N_SSD_GROUPS, D_STATE, d_inner // N_SSD_GROUPS), F32),
                        pltpu.VMEM((tm, d_inner), BF16)],
        compiler_params=_params(),
        name="ssd_mixer",
    )(x2, wpre, win, cw, cb, dtb, alog, dexp, nw, wout, wpost)


def _pad_lanes(v):
    return jnp.pad(v.astype(F32), (0, LANES - v.shape[0]))[None, :]


def kernel(x, ssd_w_in, ssd_conv_w, ssd_conv_b, ssd_dt_bias, ssd_a_log, ssd_d, ssd_norm_w, ssd_w_out, pool_w, pool_scale, ffn_w_up, ffn_conv_w, ffn_conv_b, ffn_w_down, norm_mix_pre, norm_mix_post, norm_ffn_pre, norm_ffn_post):
    batch, seq, d = x.shape
    depth = ffn_w_up.shape[0]
    n_mixers = 2
    n_heads = ssd_dt_bias.shape[1]
    assert seq % CHUNK == 0 and n_heads <= LANES
    x2 = x.reshape(batch * seq, d)
    for i in range(depth):
        j = i // n_mixers
        if i % n_mixers == 0:
            win = jnp.pad(ssd_w_in[j], ((0, 0), (0, LANES - n_heads))).astype(BF16)
            x2 = _ssd_layer(
                x2, batch, norm_mix_pre[i][None, :], win, ssd_conv_w[j], ssd_conv_b[j][None, :],
                _pad_lanes(ssd_dt_bias[j]), _pad_lanes(ssd_a_log[j]),
                jnp.repeat(ssd_d[j].astype(F32), HEAD_DIM)[None, :], ssd_norm_w[j][None, :],
                ssd_w_out[j].astype(BF16), norm_mix_post[i][None, :])
        else:
            x2 = _pool_layer(x2, batch, norm_mix_pre[i][None, :], pool_w[j].astype(BF16),
                             pool_scale[j][None, :], norm_mix_post[i][None, :])
        x2 = _ffn_layer(x2, batch, norm_ffn_pre[i][None, :], ffn_w_up[i].astype(BF16),
                        ffn_conv_w[i], ffn_conv_b[i][None, :], ffn_w_down[i].astype(BF16),
                        norm_ffn_post[i][None, :])
    return x2.reshape(batch, seq, d)
```

```python
import functools

import jax
import jax.numpy as jnp
from jax import lax
from jax.experimental import pallas as pl
from jax.experimental.pallas import tpu as pltpu

F32 = jnp.float32
BF16 = jnp.bfloat16

EPS = 1e-6
HEAD_DIM = 64
D_STATE = 128
N_SSD_GROUPS = 4
CHUNK = 128
POOL_WINDOWS = (2, 4, 8, 16)
LANES = 128
CARRY_ROWS = 8
POOL_CARRY_ROWS = 16
PHASES = 8
VMEM_LIMIT_BYTES = 56 * 1024 * 1024

FFN_TILE = 512
POOL_TILE = 256
SSD_TILE = 256
FFN_COL_BLOCK = 256
PROJ_COL_BLOCK = 512
CONV_COL_BLOCK = 256


def _rms(x, w):
    ms = jnp.mean(x * x, axis=-1, keepdims=True)
    return x * lax.rsqrt(ms + EPS) * w


def _sigmoid(x):
    return 1.0 / (1.0 + jnp.exp(-x))


def _phased_spec(tm, d, nt):
    return pl.BlockSpec((tm // PHASES, PHASES * d), lambda b, t: (b * nt + t, 0))


def _load_phased(x_ref, d, n_chunks):
    rq = x_ref.shape[0] // n_chunks
    return jnp.concatenate([x_ref[c * rq:(c + 1) * rq, p * d:(p + 1) * d]
                            for c in range(n_chunks) for p in range(PHASES)], axis=0)


def _store_phased(o_ref, val, d, n_chunks):
    rq = o_ref.shape[0] // n_chunks
    for c in range(n_chunks):
        for p in range(PHASES):
            r0 = (c * PHASES + p) * rq
            o_ref[c * rq:(c + 1) * rq, p * d:(p + 1) * d] = val[r0:r0 + rq, :]


def _time_index(r, rq):
    assert rq & (rq - 1) == 0
    return jnp.bitwise_and(r, rq - 1) * PHASES + jnp.right_shift(r, rq.bit_length() - 1)


def _causal_shifts(u, carry_ref, cols, kmax, n_chunks):
    cq = u.shape[0] // n_chunks
    rq = cq // PHASES
    row0 = lax.broadcasted_iota(jnp.int32, (rq, 1), 0) == 0
    rolled = {}
    for c in range(n_chunks):
        for j in range(kmax):
            base = c * cq + (PHASES - kmax + j) * rq
            if c == 0:
                prev_last = carry_ref[(j + 1) * CARRY_ROWS - 1:(j + 1) * CARRY_ROWS, cols]
            else:
                prev_last = u[base - cq + rq - 1:base - cq + rq, :]
            rolled[c, j] = jnp.where(row0, prev_last, pltpu.roll(u[base:base + rq, :], 1, axis=0))
    for j in range(kmax):
        end = (n_chunks - 1) * cq + (PHASES - kmax + j + 1) * rq
        carry_ref[j * CARRY_ROWS:(j + 1) * CARRY_ROWS, cols] = u[end - CARRY_ROWS:end, :]
    shifts = []
    for k in range(1, kmax + 1):
        pieces = []
        for c in range(n_chunks):
            pieces += [rolled[c, j] for j in range(kmax - k, kmax)]
            pieces.append(u[c * cq:c * cq + (PHASES - k) * rq, :])
        shifts.append(jnp.concatenate(pieces, axis=0))
    return shifts


def _split_bf16(x, parts):
    out = []
    rem = x
    for _ in range(parts):
        p = rem.astype(BF16)
        out.append(p)
        rem = rem - p.astype(F32)
    return out


def _const_spec(shape):
    nd = len(shape)
    return pl.BlockSpec(shape, lambda b, t: (0,) * nd, pipeline_mode=pl.Buffered(1))


def _row_spec(tm, d, nt):
    return pl.BlockSpec((tm, d), lambda b, t: (b * nt + t, 0))


def _params():
    return pltpu.CompilerParams(dimension_semantics=("arbitrary", "arbitrary"),
                                vmem_limit_bytes=VMEM_LIMIT_BYTES)


def _ffn_body(x_ref, wpre_ref, wup_ref, cw_ref, cb_ref, wdn_ref, wpost_ref, o_ref,
              carry_ref, act_ref, *, tm, d, dff, fc):
    @pl.when(pl.program_id(1) == 0)
    def _():
        carry_ref[...] = jnp.zeros_like(carry_ref)

    kmax = cw_ref.shape[0] - 1
    x = _load_phased(x_ref, d, 1)
    h = _rms(x, wpre_ref[...]).astype(BF16)

    def conv_half(col):
        cols = slice(col, col + fc)
        u = jnp.dot(h, wup_ref[:, cols], preferred_element_type=F32)
        s1, s2 = _causal_shifts(u, carry_ref, cols, kmax, 1)
        cw = cw_ref[:, cols]
        return cb_ref[:, cols] + cw[2:3, :] * u + cw[1:2, :] * s1 + cw[0:1, :] * s2

    for c in range(dff // fc):
        g = conv_half(c * fc)
        v = conv_half(dff + c * fc)
        act_ref[:, c * fc:(c + 1) * fc] = (g * _sigmoid(g) * v).astype(BF16)

    f = jnp.dot(act_ref[...], wdn_ref[...], preferred_element_type=F32)
    _store_phased(o_ref, x + _rms(f, wpost_ref[...]), d, 1)


def _ffn_layer(x2, batch, wpre, wup, cw, cb, wdn, wpost):
    m, d = x2.shape
    seq = m // batch
    tm = min(FFN_TILE, seq)
    nt = seq // tm
    dff = wdn.shape[0]
    kmax = cw.shape[0] - 1
    body = functools.partial(_ffn_body, tm=tm, d=d, dff=dff, fc=FFN_COL_BLOCK)
    out = pl.pallas_call(
        body,
        grid=(batch, nt),
        in_specs=[_phased_spec(tm, d, nt), _const_spec((1, d)), _const_spec(wup.shape),
                  _const_spec(cw.shape), _const_spec(cb.shape), _const_spec(wdn.shape),
                  _const_spec((1, d))],
        out_specs=_phased_spec(tm, d, nt),
        out_shape=jax.ShapeDtypeStruct((m // PHASES, PHASES * d), F32),
        scratch_shapes=[pltpu.VMEM((kmax * CARRY_ROWS, 2 * dff), F32), pltpu.VMEM((tm, dff), BF16)],
        compiler_params=_params(),
        name="conv_ffn",
    )(x2.reshape(m // PHASES, PHASES * d), wpre, wup, cw, cb, wdn, wpost)
    return out.reshape(m, d)


def _pool_body(x_ref, wpre_ref, wpool_ref, scale_ref, wpost_ref, o_ref, carry_ref, *, tm, dg):
    t = pl.program_id(1)

    @pl.when(t == 0)
    def _():
        carry_ref[...] = jnp.zeros_like(carry_ref)

    x = x_ref[...]
    h = _rms(x, wpre_ref[...])
    prev = carry_ref[...]
    carry_ref[...] = h[tm - POOL_CARRY_ROWS:, :]
    pos1 = (t * tm + 1 + lax.broadcasted_iota(jnp.int32, (tm, 1), 0)).astype(F32)

    outs = []
    for k, w in enumerate(POOL_WINDOWS):
        hg = h[:, k * dg:(k + 1) * dg]
        s = jnp.concatenate([prev[:, k * dg:(k + 1) * dg], hg], axis=0)
        step = 1
        while step < w:
            s = s + pltpu.roll(s, step, axis=0)
            step *= 2
        mean = s[POOL_CARRY_ROWS:, :] / jnp.minimum(pos1, float(w))
        mixed = (mean - hg).astype(BF16)
        outs.append(jnp.dot(mixed, wpool_ref[k], preferred_element_type=F32))
    out = jnp.concatenate(outs, axis=1) * scale_ref[...]
    o_ref[...] = x + _rms(out, wpost_ref[...])


def _pool_layer(x2, batch, wpre, wpool, scale, wpost):
    m, d = x2.shape
    seq = m // batch
    tm = min(POOL_TILE, seq)
    nt = seq // tm
    dg = wpool.shape[1]
    body = functools.partial(_pool_body, tm=tm, dg=dg)
    return pl.pallas_call(
        body,
        grid=(batch, nt),
        in_specs=[_row_spec(tm, d, nt), _const_spec((1, d)), _const_spec(wpool.shape),
                  _const_spec((1, d)), _const_spec((1, d))],
        out_specs=_row_spec(tm, d, nt),
        out_shape=jax.ShapeDtypeStruct((m, d), F32),
        scratch_shapes=[pltpu.VMEM((POOL_CARRY_ROWS, d), F32)],
        compiler_params=_params(),
        name="pool_mixer",
    )(x2, wpre, wpool, scale, wpost)


def _ssd_body(x_ref, wpre_ref, win_ref, cw_ref, cb_ref, dtb_ref, alog_ref, dexp_ref, nw_ref,
              wout_ref, wpost_ref, o_ref, proj_ref, carry_ref, state_ref, y_ref,
              *, tm, d, d_inner):
    q = CHUNK
    n = D_STATE
    n_chunks = tm // q
    rq = q // PHASES
    gn = N_SSD_GROUPS * n
    gw = d_inner // N_SSD_GROUPS
    d_xbc = d_inner + 2 * gn
    off_x = d_inner
    off_b = off_x + d_inner
    off_c = off_b + gn
    off_dt = off_c + gn
    d_proj = off_dt + LANES
    kmax = cw_ref.shape[0] - 1

    @pl.when(pl.program_id(1) == 0)
    def _():
        carry_ref[...] = jnp.zeros_like(carry_ref)
        state_ref[...] = jnp.zeros_like(state_ref)

    x = _load_phased(x_ref, d, n_chunks)
    h = _rms(x, wpre_ref[...]).astype(BF16)
    for c0 in range(0, d_proj, PROJ_COL_BLOCK):
        c1 = min(c0 + PROJ_COL_BLOCK, d_proj)
        proj_ref[:, c0:c1] = jnp.dot(h, win_ref[:, c0:c1], preferred_element_type=F32)

    for c0 in range(0, d_xbc, CONV_COL_BLOCK):
        cols = slice(c0, c0 + CONV_COL_BLOCK)
        pcols = slice(off_x + c0, off_x + c0 + CONV_COL_BLOCK)
        u = proj_ref[:, pcols]
        s1, s2, s3 = _causal_shifts(u, carry_ref, cols, kmax, n_chunks)
        cw = cw_ref[:, cols]
        y = cb_ref[:, cols] + cw[3:4, :] * u + cw[2:3, :] * s1 + cw[1:2, :] * s2 + cw[0:1, :] * s3
        proj_ref[:, pcols] = y * _sigmoid(y)

    a_neg = -jnp.exp(alog_ref[...])
    ri = lax.broadcasted_iota(jnp.int32, (q, q), 0)
    ci = lax.broadcasted_iota(jnp.int32, (q, q), 1)
    t_row = _time_index(ri, rq)
    t_col = _time_index(ci, rq)
    causal = t_row >= t_col
    incl_t = jnp.where(t_row <= t_col, 1.0, 0.0).astype(BF16)
    lane = lax.broadcasted_iota(jnp.int32, (q, LANES), 1)
    left = lane < HEAD_DIM
    last = lane == q - 1

    for ch in range(n_chunks):
        r0, r1 = ch * q, (ch + 1) * q
        v = proj_ref[r0:r1, off_dt:off_dt + LANES] + dtb_ref[...]
        dt = jnp.maximum(v, 0.0) + jnp.log1p(jnp.exp(-jnp.abs(v)))
        dt_t = dt.T
        a_dt_t = (dt * a_neg).T
        a_cum_t = sum(jnp.dot(p, incl_t, preferred_element_type=F32)
                      for p in _split_bf16(a_dt_t, 3))
        a_cum = a_cum_t.T
        a_last_t = jnp.sum(jnp.where(last, a_cum_t, 0.0), axis=1, keepdims=True)
        w_end_t = dt_t * jnp.exp(a_last_t - a_cum_t)
        seg_sub_t = a_cum_t - jnp.log(dt_t)

        for g in range(N_SSD_GROUPS):
            bt = proj_ref[r0:r1, off_b + g * n:off_b + (g + 1) * n].T
            cg = proj_ref[r0:r1, off_c + g * n:off_c + (g + 1) * n].astype(BF16)
            cb = jnp.dot(cg, bt.astype(BF16), preferred_element_type=F32)
            s_old = state_ref[g]
            y_off = jnp.dot(cg, s_old.astype(BF16), preferred_element_type=F32)

            pieces = []
            for j in range(gw // LANES):
                c0 = g * gw + j * LANES
                ms, bws, e_cols = [], [], []
                for hh in range(2):
                    head = c0 // HEAD_DIM + hh
                    col = jnp.sum(jnp.where(lane == head, a_cum, 0.0), axis=1, keepdims=True)
                    seg = col - seg_sub_t[head:head + 1, :]
                    ms.append((cb * jnp.exp(jnp.where(causal, seg, -jnp.inf))).astype(BF16))
                    bws.append((bt * w_end_t[head:head + 1, :]).astype(BF16))
                    e_cols.append(jnp.exp(col))
                xp = proj_ref[r0:r1, off_x + c0:off_x + c0 + LANES]
                rhs = jnp.concatenate([jnp.where(left, xp, 0.0), jnp.where(left, 0.0, xp)],
                                      axis=0).astype(BF16)
                lhs = jnp.concatenate([jnp.concatenate(ms, axis=1),
                                       jnp.concatenate(bws, axis=1)], axis=0)
                res = jnp.dot(lhs, rhs, preferred_element_type=F32)
                e_pair = jnp.where(left, e_cols[0], e_cols[1])
                y = (res[0:q] + y_off[:, j * LANES:(j + 1) * LANES] * e_pair
                     + dexp_ref[:, c0:c0 + LANES] * xp)
                state_ref[g, :, j * LANES:(j + 1) * LANES] = (
                    s_old[:, j * LANES:(j + 1) * LANES] * e_pair[q - 1:q, :] + res[q:2 * q])
                z = proj_ref[r0:r1, c0:c0 + LANES]
                pieces.append(y * (z * _sigmoid(z)))
            yg = jnp.concatenate(pieces, axis=1)
            yg = yg * lax.rsqrt(jnp.mean(yg * yg, axis=-1, keepdims=True) + EPS)
            y_ref[r0:r1, g * gw:(g + 1) * gw] = (yg * nw_ref[:, g * gw:(g + 1) * gw]).astype(BF16)

    out = jnp.dot(y_ref[...], wout_ref[...], preferred_element_type=F32)
    _store_phased(o_ref, x + _rms(out, wpost_ref[...]), d, n_chunks)


def _ssd_layer(x2, batch, wpre, win, cw, cb, dtb, alog, dexp, nw, wout, wpost):
    m, d = x2.shape
    seq = m // batch
    tm = min(SSD_TILE, seq)
    nt = seq // tm
    d_inner = wout.shape[0]
    d_xbc = cw.shape[1]
    kmax = cw.shape[0] - 1
    body = functools.partial(_ssd_body, tm=tm, d=d, d_inner=d_inner)
    out = pl.pallas_call(
        body,
        grid=(batch, nt),
        in_specs=[_phased_spec(tm, d, nt), _const_spec((1, d)), _const_spec(win.shape),
                  _const_spec(cw.shape), _const_spec(cb.shape), _const_spec(dtb.shape),
                  _const_spec(alog.shape), _const_spec(dexp.shape), _const_spec(nw.shape),
                  _const_spec(wout.shape), _const_spec((1, d))],
        out_specs=_phased_spec(tm, d, nt),
        out_shape=jax.ShapeDtypeStruct((m // PHASES, PHASES * d), F32),
        scratch_shapes=[pltpu.VMEM((tm, win.shape[1]), F32),
                        pltpu.VMEM((kmax * CARRY_ROWS, d_xbc), F32),
                        pltpu.VMEM((N_SSD_GROUPS, D_STATE, d_inner // N_SSD_GROUPS), F32),
                        pltpu.VMEM((tm, d_inner), BF16)],
        compiler_params=_params(),
        name="ssd_mixer",
    )(x2.reshape(m // PHASES, PHASES * d), wpre, win, cw, cb, dtb, alog, dexp, nw, wout, wpost)
    return out.reshape(m, d)


def _pad_lanes(v):
    return jnp.pad(v.astype(F32), (0, LANES - v.shape[0]))[None, :]


def kernel(x, ssd_w_in, ssd_conv_w, ssd_conv_b, ssd_dt_bias, ssd_a_log, ssd_d, ssd_norm_w, ssd_w_out, pool_w, pool_scale, ffn_w_up, ffn_conv_w, ffn_conv_b, ffn_w_down, norm_mix_pre, norm_mix_post, norm_ffn_pre, norm_ffn_post):
    batch, seq, d = x.shape
    depth = ffn_w_up.shape[0]
    n_mixers = 2
    n_heads = ssd_dt_bias.shape[1]
    assert seq % CHUNK == 0 and n_heads <= LANES
    x2 = x.reshape(batch * seq, d)
    for i in range(depth):
        j = i // n_mixers
        if i % n_mixers == 0:
            win = jnp.pad(ssd_w_in[j], ((0, 0), (0, LANES - n_heads))).astype(BF16)
            x2 = _ssd_layer(
                x2, batch, norm_mix_pre[i][None, :], win, ssd_conv_w[j], ssd_conv_b[j][None, :],
                _pad_lanes(ssd_dt_bias[j]), _pad_lanes(ssd_a_log[j]),
                jnp.repeat(ssd_d[j].astype(F32), HEAD_DIM)[None, :], ssd_norm_w[j][None, :],
                ssd_w_out[j].astype(BF16), norm_mix_post[i][None, :])
        else:
            x2 = _pool_layer(x2, batch, norm_mix_pre[i][None, :], pool_w[j].astype(BF16),
                             pool_scale[j][None, :], norm_mix_post[i][None, :])
        x2 = _ffn_layer(x2, batch, norm_ffn_pre[i][None, :], ffn_w_up[i].astype(BF16),
                        ffn_conv_w[i], ffn_conv_b[i][None, :], ffn_w_down[i].astype(BF16),
                        norm_ffn_post[i][None, :])
    return x2.reshape(batch, seq, d)
```

```python
import functools

import jax
import jax.numpy as jnp
from jax import lax
from jax.experimental import pallas as pl
from jax.experimental.pallas import tpu as pltpu

F32 = jnp.float32
BF16 = jnp.bfloat16

EPS = 1e-6
LOG2E = 1.4426950408889634
HEAD_DIM = 64
D_STATE = 128
N_SSD_GROUPS = 4
CHUNK = 128
POOL_WINDOWS = (2, 4, 8, 16)
LANES = 128
CARRY_ROWS = 8
POOL_CARRY_ROWS = 16
PHASES = 8
VMEM_LIMIT_BYTES = 56 * 1024 * 1024

FFN_TILE = 1024
POOL_TILE = 512
SSD_TILE = 512
FFN_COL_BLOCK = 256
PROJ_COL_BLOCK = 512
CONV_COL_BLOCK = 256
OUT_COL_BLOCK = 256


def _rms(x, w):
    ms = jnp.mean(x * x, axis=-1, keepdims=True)
    return x * lax.rsqrt(ms + EPS) * w


def _sigmoid(x):
    return 1.0 / (1.0 + jnp.exp2(x * -LOG2E))


def _to_slabs(slab_ref, x_ref):
    for k in range(slab_ref.shape[0]):
        slab_ref[k] = x_ref[:, k * LANES:(k + 1) * LANES]


def _from_slabs(o_ref, slab_ref):
    for k in range(slab_ref.shape[0]):
        o_ref[:, k * LANES:(k + 1) * LANES] = slab_ref[k]


def _load_phased(slab_ref, row0, rows):
    rq = rows // PHASES
    return jnp.concatenate(
        [jnp.concatenate([slab_ref[k, pl.ds(row0 + p, rq, stride=PHASES), :]
                          for k in range(slab_ref.shape[0])], axis=1)
         for p in range(PHASES)], axis=0)


def _store_phased(slab_ref, row0, val):
    rq = val.shape[0] // PHASES
    for p in range(PHASES):
        for k in range(slab_ref.shape[0]):
            slab_ref[k, pl.ds(row0 + p, rq, stride=PHASES), :] = (
                val[p * rq:(p + 1) * rq, k * LANES:(k + 1) * LANES])


def _time_index(r, rq):
    assert rq & (rq - 1) == 0
    return jnp.bitwise_and(r, rq - 1) * PHASES + jnp.right_shift(r, rq.bit_length() - 1)


def _causal_shifts(u, carry_ref, cols, kmax):
    rq = u.shape[0] // PHASES
    row0 = lax.broadcasted_iota(jnp.int32, (rq, 1), 0) == 0
    rolled = []
    for j in range(kmax):
        base = (PHASES - kmax + j) * rq
        prev_last = carry_ref[(j + 1) * CARRY_ROWS - 1:(j + 1) * CARRY_ROWS, cols]
        rolled.append(jnp.where(row0, prev_last, pltpu.roll(u[base:base + rq, :], 1, axis=0)))
        carry_ref[j * CARRY_ROWS:(j + 1) * CARRY_ROWS, cols] = u[base + rq - CARRY_ROWS:base + rq, :]
    return [jnp.concatenate(rolled[kmax - k:] + [u[:(PHASES - k) * rq, :]], axis=0)
            for k in range(1, kmax + 1)]


def _split_bf16(x, parts):
    out = []
    rem = x
    for _ in range(parts):
        p = rem.astype(BF16)
        out.append(p)
        rem = rem - p.astype(F32)
    return out


def _spread(major, minor):
    out, done = [], 0
    for i, piece in enumerate(major):
        out.append(piece)
        want = (i + 1) * len(minor) // len(major)
        out.extend(minor[done:want])
        done = want
    return out


def _run(pieces):
    for piece in pieces:
        piece()


def _const_spec(shape):
    nd = len(shape)
    return pl.BlockSpec(shape, lambda b, t: (0,) * nd, pipeline_mode=pl.Buffered(1))


def _row_spec(tm, d, nt):
    return pl.BlockSpec((tm, d), lambda b, t: (b * nt + t, 0))


def _slab_spec(tm, d, nt):
    return pl.BlockSpec((d // LANES, tm, LANES), lambda b, t: (0, b * nt + t, 0))


def _stream_spec(slabs, tm, d, nt):
    return _slab_spec(tm, d, nt) if slabs else _row_spec(tm, d, nt)


def _stream_shape(slabs, m, d):
    return jax.ShapeDtypeStruct((d // LANES, m, LANES) if slabs else (m, d), F32)


def _params():
    return pltpu.CompilerParams(dimension_semantics=("arbitrary", "arbitrary"),
                                vmem_limit_bytes=VMEM_LIMIT_BYTES)


def _ffn_body(x_ref, wpre_ref, wup_ref, cw_ref, cb_ref, wdn_ref, wpost_ref, o_ref,
              carry_ref, act_ref, *stage_refs, tm, dff, fc, slabs_out):
    @pl.when(pl.program_id(1) == 0)
    def _():
        carry_ref[...] = jnp.zeros_like(carry_ref)

    kmax = cw_ref.shape[0] - 1
    x = _load_phased(x_ref, 0, tm)
    h = _rms(x, wpre_ref[...]).astype(BF16)

    def conv_half(col):
        cols = slice(col, col + fc)
        u = jnp.dot(h, wup_ref[:, cols], preferred_element_type=F32)
        s1, s2 = _causal_shifts(u, carry_ref, cols, kmax)
        cw = cw_ref[:, cols]
        return cb_ref[:, cols] + cw[2:3, :] * u + cw[1:2, :] * s1 + cw[0:1, :] * s2

    for c in range(dff // fc):
        g = conv_half(c * fc)
        v = conv_half(dff + c * fc)
        act_ref[:, c * fc:(c + 1) * fc] = (g * _sigmoid(g) * v).astype(BF16)

    f = jnp.dot(act_ref[...], wdn_ref[...], preferred_element_type=F32)
    res = x + _rms(f, wpost_ref[...])
    if slabs_out:
        _store_phased(o_ref, 0, res)
    else:
        _store_phased(stage_refs[0], 0, res)
        _from_slabs(o_ref, stage_refs[0])


def _ffn_layer(xs, batch, wpre, wup, cw, cb, wdn, wpost, slabs_out):
    nl, m, _ = xs.shape
    d = nl * LANES
    seq = m // batch
    tm = min(FFN_TILE, seq)
    nt = seq // tm
    dff = wdn.shape[0]
    kmax = cw.shape[0] - 1
    body = functools.partial(_ffn_body, tm=tm, dff=dff, fc=FFN_COL_BLOCK, slabs_out=slabs_out)
    scratch = [pltpu.VMEM((kmax * CARRY_ROWS, 2 * dff), F32), pltpu.VMEM((tm, dff), BF16)]
    if not slabs_out:
        scratch.append(pltpu.VMEM((nl, tm, LANES), F32))
    return pl.pallas_call(
        body,
        grid=(batch, nt),
        in_specs=[_slab_spec(tm, d, nt), _const_spec((1, d)), _const_spec(wup.shape),
                  _const_spec(cw.shape), _const_spec(cb.shape), _const_spec(wdn.shape),
                  _const_spec((1, d))],
        out_specs=_stream_spec(slabs_out, tm, d, nt),
        out_shape=_stream_shape(slabs_out, m, d),
        scratch_shapes=scratch,
        compiler_params=_params(),
        name="conv_ffn",
    )(xs, wpre, wup, cw, cb, wdn, wpost)


def _pool_body(x_ref, wpre_ref, wpool_ref, scale_ref, wpost_ref, o_ref, carry_ref, *, tm, dg):
    t = pl.program_id(1)

    @pl.when(t == 0)
    def _():
        carry_ref[...] = jnp.zeros_like(carry_ref)

    nl = x_ref.shape[0]
    x = jnp.concatenate([x_ref[k] for k in range(nl)], axis=1)
    h = _rms(x, wpre_ref[...])
    prev = carry_ref[...]
    carry_ref[...] = h[tm - POOL_CARRY_ROWS:, :]
    pos1 = (t * tm + 1 + lax.broadcasted_iota(jnp.int32, (tm, 1), 0)).astype(F32)

    outs = []
    for k, w in enumerate(POOL_WINDOWS):
        hg = h[:, k * dg:(k + 1) * dg]
        s = jnp.concatenate([prev[:, k * dg:(k + 1) * dg], hg], axis=0)
        step = 1
        while step < w:
            s = s + pltpu.roll(s, step, axis=0)
            step *= 2
        mean = s[POOL_CARRY_ROWS:, :] / jnp.minimum(pos1, float(w))
        mixed = (mean - hg).astype(BF16)
        outs.append(jnp.dot(mixed, wpool_ref[k], preferred_element_type=F32))
    out = jnp.concatenate(outs, axis=1) * scale_ref[...]
    res = x + _rms(out, wpost_ref[...])
    for k in range(nl):
        o_ref[k] = res[:, k * LANES:(k + 1) * LANES]


def _pool_layer(xs, batch, wpre, wpool, scale, wpost):
    nl, m, _ = xs.shape
    d = nl * LANES
    seq = m // batch
    tm = min(POOL_TILE, seq)
    nt = seq // tm
    dg = wpool.shape[1]
    body = functools.partial(_pool_body, tm=tm, dg=dg)
    return pl.pallas_call(
        body,
        grid=(batch, nt),
        in_specs=[_slab_spec(tm, d, nt), _const_spec((1, d)), _const_spec(wpool.shape),
                  _const_spec((1, d)), _const_spec((1, d))],
        out_specs=_slab_spec(tm, d, nt),
        out_shape=_stream_shape(True, m, d),
        scratch_shapes=[pltpu.VMEM((POOL_CARRY_ROWS, d), F32)],
        compiler_params=_params(),
        name="pool_mixer",
    )(xs, wpre, wpool, scale, wpost)


def _ssd_body(x_ref, wpre_ref, win_ref, cw_ref, cb_ref, dtb_ref, alog_ref, dexp_ref, nw_ref,
              wout_ref, wpost_ref, o_ref, proj0_ref, proj1_ref, y0_ref, y1_ref, carry_ref,
              state_ref, *, tm, d_inner):
    q = CHUNK
    n = D_STATE
    n_chunks = tm // q
    rq = q // PHASES
    gn = N_SSD_GROUPS * n
    gw = d_inner // N_SSD_GROUPS
    d_xbc = d_inner + 2 * gn
    off_x = d_inner
    off_b = off_x + d_inner
    off_c = off_b + gn
    off_dt = off_c + gn
    d_proj = off_dt + LANES
    d_out = wout_ref.shape[1]
    kmax = cw_ref.shape[0] - 1
    proj_refs = (proj0_ref, proj1_ref)
    y_refs = (y0_ref, y1_ref)

    @pl.when(pl.program_id(1) == 0)
    def _():
        carry_ref[...] = jnp.zeros_like(carry_ref)
        state_ref[...] = jnp.zeros_like(state_ref)

    a_neg = -jnp.exp(alog_ref[...]) * LOG2E
    ri = lax.broadcasted_iota(jnp.int32, (q, q), 0)
    ci = lax.broadcasted_iota(jnp.int32, (q, q), 1)
    t_row = _time_index(ri, rq)
    t_col = _time_index(ci, rq)
    causal = t_row >= t_col
    incl_t = jnp.where(t_row <= t_col, 1.0, 0.0).astype(BF16)
    lane = lax.broadcasted_iota(jnp.int32, (q, LANES), 1)
    left = lane < HEAD_DIM
    last = lane == q - 1


    def project(ch):
        proj_ref = proj_refs[ch % 2]
        ctx = {}

        def norm():
            ctx["h"] = _rms(_load_phased(x_ref, ch * q, q), wpre_ref[...]).astype(BF16)

        def block(c0):
            c1 = min(c0 + PROJ_COL_BLOCK, d_proj)
            proj_ref[:, c0:c1] = jnp.dot(ctx["h"], win_ref[:, c0:c1], preferred_element_type=F32)

        return [norm] + [functools.partial(block, c0) for c0 in range(0, d_proj, PROJ_COL_BLOCK)]

    def conv(ch):
        proj_ref = proj_refs[ch % 2]

        def block(c0):
            cols = slice(c0, c0 + CONV_COL_BLOCK)
            pcols = slice(off_x + c0, off_x + c0 + CONV_COL_BLOCK)
            u = proj_ref[:, pcols]
            s1, s2, s3 = _causal_shifts(u, carry_ref, cols, kmax)
            cw = cw_ref[:, cols]
            y = (cb_ref[:, cols] + cw[3:4, :] * u + cw[2:3, :] * s1 + cw[1:2, :] * s2
                 + cw[0:1, :] * s3)
            proj_ref[:, pcols] = y * _sigmoid(y)

        return [functools.partial(block, c0) for c0 in range(0, d_xbc, CONV_COL_BLOCK)]

    def scan(ch):
        proj_ref = proj_refs[ch % 2]
        y_ref = y_refs[ch % 2]
        ctx = {}

        def prelude():
            v = proj_ref[:, off_dt:off_dt + LANES] + dtb_ref[...]
            dt = jnp.maximum(v, 0.0) + jnp.log1p(jnp.exp(-jnp.abs(v)))
            dt_t = dt.T
            a_dt_t = (dt * a_neg).T
            a_cum_t = sum(jnp.dot(p, incl_t, preferred_element_type=F32)
                          for p in _split_bf16(a_dt_t, 3))
            ctx["a_cum"] = a_cum_t.T
            a_last_t = jnp.sum(jnp.where(last, a_cum_t, 0.0), axis=1, keepdims=True)
            ctx["w_end_t"] = dt_t * jnp.exp2(a_last_t - a_cum_t)
            ctx["seg_sub_t"] = a_cum_t - jnp.log2(dt_t)

        def group_start(g):
            bt = proj_ref[:, off_b + g * n:off_b + (g + 1) * n].T
            cg = proj_ref[:, off_c + g * n:off_c + (g + 1) * n].astype(BF16)
            ctx["bt"] = bt
            ctx["cb"] = jnp.dot(cg, bt.astype(BF16), preferred_element_type=F32)
            ctx["s_old"] = state_ref[g]
            ctx["y_off"] = jnp.dot(cg, ctx["s_old"].astype(BF16), preferred_element_type=F32)
            ctx["pieces"] = []

        def pair(g, j):
            c0 = g * gw + j * LANES
            ms, bws, e_cols = [], [], []
            for hh in range(2):
                head = c0 // HEAD_DIM + hh
                col = jnp.sum(jnp.where(lane == head, ctx["a_cum"], 0.0), axis=1, keepdims=True)
                seg = col - ctx["seg_sub_t"][head:head + 1, :]
                ms.append((ctx["cb"] * jnp.exp2(jnp.where(causal, seg, -jnp.inf))).astype(BF16))
                bws.append((ctx["bt"] * ctx["w_end_t"][head:head + 1, :]).astype(BF16))
                e_cols.append(jnp.exp2(col))
            xp = proj_ref[:, off_x + c0:off_x + c0 + LANES]
            rhs = jnp.concatenate([jnp.where(left, xp, 0.0), jnp.where(left, 0.0, xp)],
                                  axis=0).astype(BF16)
            lhs = jnp.concatenate([jnp.concatenate(ms, axis=1),
                                   jnp.concatenate(bws, axis=1)], axis=0)
            res = jnp.dot(lhs, rhs, preferred_element_type=F32)
            e_pair = jnp.where(left, e_cols[0], e_cols[1])
            y = (res[0:q] + ctx["y_off"][:, j * LANES:(j + 1) * LANES] * e_pair
                 + dexp_ref[:, c0:c0 + LANES] * xp)
            state_ref[g, :, j * LANES:(j + 1) * LANES] = (
                ctx["s_old"][:, j * LANES:(j + 1) * LANES] * e_pair[q - 1:q, :] + res[q:2 * q])
            z = proj_ref[:, c0:c0 + LANES]
            ctx["pieces"].append(y * (z * _sigmoid(z)))

        def group_end(g):
            yg = jnp.concatenate(ctx["pieces"], axis=1)
            yg = yg * lax.rsqrt(jnp.mean(yg * yg, axis=-1, keepdims=True) + EPS)
            y_ref[:, g * gw:(g + 1) * gw] = (yg * nw_ref[:, g * gw:(g + 1) * gw]).astype(BF16)

        pieces = [prelude]
        for g in range(N_SSD_GROUPS):
            pieces.append(functools.partial(group_start, g))
            pieces += [functools.partial(pair, g, j) for j in range(gw // LANES)]
            pieces.append(functools.partial(group_end, g))
        return pieces

    def finish(ch):
        y_ref = y_refs[ch % 2]
        outs = []

        def block(c0):
            outs.append(jnp.dot(y_ref[...], wout_ref[:, c0:c0 + OUT_COL_BLOCK],
                                preferred_element_type=F32))

        def store():
            x = _load_phased(x_ref, ch * q, q)
            res = x + _rms(jnp.concatenate(outs, axis=1), wpost_ref[...])
            _store_phased(o_ref, ch * q, res)

        return [functools.partial(block, c0) for c0 in range(0, d_out, OUT_COL_BLOCK)] + [store]

    _run(project(0) + conv(0))
    for ch in range(n_chunks):
        more = ch + 1 < n_chunks
        _run(_spread(scan(ch), project(ch + 1) if more else []))
        _run(_spread(finish(ch), conv(ch + 1) if more else []))


def _ssd_layer(xin, batch, slabs_in, wpre, win, cw, cb, dtb, alog, dexp, nw, wout, wpost):
    if slabs_in:
        nl, m, _ = xin.shape
        d = nl * LANES
    else:
        m, d = xin.shape
        nl = d // LANES
    seq = m // batch
    tm = min(SSD_TILE, seq)
    nt = seq // tm
    d_inner = wout.shape[0]
    d_xbc = cw.shape[1]
    kmax = cw.shape[0] - 1
    body = functools.partial(_ssd_body, tm=tm, d_inner=d_inner)
    scratch = [pltpu.VMEM((CHUNK, win.shape[1]), F32),
               pltpu.VMEM((CHUNK, win.shape[1]), F32),
               pltpu.VMEM((CHUNK, d_inner), BF16),
               pltpu.VMEM((CHUNK, d_inner), BF16),
               pltpu.VMEM((kmax * CARRY_ROWS, d_xbc), F32),
               pltpu.VMEM((N_SSD_GROUPS, D_STATE, d_inner // N_SSD_GROUPS), F32)]
    if not slabs_in:
        body = functools.partial(_ssd_body_natural_in, tm=tm, d_inner=d_inner)
        scratch.append(pltpu.VMEM((nl, tm, LANES), F32))
    return pl.pallas_call(
        body,
        grid=(batch, nt),
        in_specs=[_stream_spec(slabs_in, tm, d, nt), _const_spec((1, d)), _const_spec(win.shape),
                  _const_spec(cw.shape), _const_spec(cb.shape), _const_spec(dtb.shape),
                  _const_spec(alog.shape), _const_spec(dexp.shape), _const_spec(nw.shape),
                  _const_spec(wout.shape), _const_spec((1, d))],
        out_specs=_slab_spec(tm, d, nt),
        out_shape=_stream_shape(True, m, d),
        scratch_shapes=scratch,
        compiler_params=_params(),
        name="ssd_mixer",
    )(xin, wpre, win, cw, cb, dtb, alog, dexp, nw, wout, wpost)


def _ssd_body_natural_in(x_ref, *refs, tm, d_inner):
    stage_ref = refs[-1]
    _to_slabs(stage_ref, x_ref)
    _ssd_body(stage_ref, *refs[:-1], tm=tm, d_inner=d_inner)


def _pad_lanes(v):
    return jnp.pad(v.astype(F32), (0, LANES - v.shape[0]))[None, :]


def kernel(x, ssd_w_in, ssd_conv_w, ssd_conv_b, ssd_dt_bias, ssd_a_log, ssd_d, ssd_norm_w, ssd_w_out, pool_w, pool_scale, ffn_w_up, ffn_conv_w, ffn_conv_b, ffn_w_down, norm_mix_pre, norm_mix_post, norm_ffn_pre, norm_ffn_post):
    batch, seq, d = x.shape
    depth = ffn_w_up.shape[0]
    n_mixers = 2
    n_heads = ssd_dt_bias.shape[1]
    assert seq % CHUNK == 0 and n_heads <= LANES and depth % n_mixers == 0
    xs = x.reshape(batch * seq, d)
    for i in range(depth):
        j = i // n_mixers
        if i % n_mixers == 0:
            win = jnp.pad(ssd_w_in[j], ((0, 0), (0, LANES - n_heads))).astype(BF16)
            xs = _ssd_layer(
                xs, batch, i > 0, norm_mix_pre[i][None, :], win, ssd_conv_w[j],
                ssd_conv_b[j][None, :], _pad_lanes(ssd_dt_bias[j]), _pad_lanes(ssd_a_log[j]),
                jnp.repeat(ssd_d[j].astype(F32), HEAD_DIM)[None, :], ssd_norm_w[j][None, :],
                ssd_w_out[j].astype(BF16), norm_mix_post[i][None, :])
        else:
            xs = _pool_layer(xs, batch, norm_mix_pre[i][None, :], pool_w[j].astype(BF16),
                             pool_scale[j][None, :], norm_mix_post[i][None, :])
        xs = _ffn_layer(xs, batch, norm_ffn_pre[i][None, :], ffn_w_up[i].astype(BF16),
                        ffn_conv_w[i], ffn_conv_b[i][None, :], ffn_w_down[i].astype(BF16),
                        norm_ffn_post[i][None, :], slabs_out=i + 1 < depth)
    return xs.reshape(batch, seq, d)
```

```python
import functools

import jax
import jax.numpy as jnp
from jax import lax
from jax.experimental import pallas as pl
from jax.experimental.pallas import tpu as pltpu

F32 = jnp.float32
BF16 = jnp.bfloat16

EPS = 1e-6
LOG2E = 1.4426950408889634
HEAD_DIM = 64
D_STATE = 128
N_SSD_GROUPS = 4
CHUNK = 128
POOL_WINDOWS = (2, 4, 8, 16)
LANES = 128
CARRY_ROWS = 8
POOL_CARRY_ROWS = 16
PHASES = 8
VMEM_LIMIT_BYTES = 56 * 1024 * 1024

FFN_TILE = 1024
POOL_TILE = 512
SSD_TILE = 1024
FFN_COL_BLOCK = 256
PROJ_COL_BLOCK = 512
CONV_COL_BLOCK = 256
OUT_COL_BLOCK = 256


def _rms(x, w):
    ms = jnp.mean(x * x, axis=-1, keepdims=True)
    return x * lax.rsqrt(ms + EPS) * w


def _sigmoid(x):
    return 1.0 / (1.0 + jnp.exp2(x * -LOG2E))


def _to_slabs(slab_ref, x_ref):
    for k in range(slab_ref.shape[0]):
        slab_ref[k] = x_ref[:, k * LANES:(k + 1) * LANES]


def _from_slabs(o_ref, slab_ref):
    for k in range(slab_ref.shape[0]):
        o_ref[:, k * LANES:(k + 1) * LANES] = slab_ref[k]


def _load_phased(slab_ref, row0, rows):
    rq = rows // PHASES
    return jnp.concatenate(
        [jnp.concatenate([slab_ref[k, pl.ds(row0 + p, rq, stride=PHASES), :]
                          for k in range(slab_ref.shape[0])], axis=1)
         for p in range(PHASES)], axis=0)


def _store_phased(slab_ref, row0, val):
    rq = val.shape[0] // PHASES
    for p in range(PHASES):
        for k in range(slab_ref.shape[0]):
            slab_ref[k, pl.ds(row0 + p, rq, stride=PHASES), :] = (
                val[p * rq:(p + 1) * rq, k * LANES:(k + 1) * LANES])


def _time_index(r, rq):
    assert rq & (rq - 1) == 0
    return jnp.bitwise_and(r, rq - 1) * PHASES + jnp.right_shift(r, rq.bit_length() - 1)


def _causal_shifts(u, carry_ref, cols, kmax):
    rq = u.shape[0] // PHASES
    row0 = lax.broadcasted_iota(jnp.int32, (rq, 1), 0) == 0
    rolled = []
    for j in range(kmax):
        base = (PHASES - kmax + j) * rq
        prev_last = carry_ref[(j + 1) * CARRY_ROWS - 1:(j + 1) * CARRY_ROWS, cols]
        rolled.append(jnp.where(row0, prev_last, pltpu.roll(u[base:base + rq, :], 1, axis=0)))
        carry_ref[j * CARRY_ROWS:(j + 1) * CARRY_ROWS, cols] = u[base + rq - CARRY_ROWS:base + rq, :]
    return [jnp.concatenate(rolled[kmax - k:] + [u[:(PHASES - k) * rq, :]], axis=0)
            for k in range(1, kmax + 1)]


def _split_bf16(x, parts):
    out = []
    rem = x
    for _ in range(parts):
        p = rem.astype(BF16)
        out.append(p)
        rem = rem - p.astype(F32)
    return out


def _spread(major, minor):
    out, done = [], 0
    for i, piece in enumerate(major):
        out.append(piece)
        want = (i + 1) * len(minor) // len(major)
        out.extend(minor[done:want])
        done = want
    return out


def _run(pieces):
    for piece in pieces:
        piece()


def _const_spec(shape):
    nd = len(shape)
    return pl.BlockSpec(shape, lambda b, t: (0,) * nd, pipeline_mode=pl.Buffered(1))


def _row_spec(tm, d, nt):
    return pl.BlockSpec((tm, d), lambda b, t: (b * nt + t, 0))


def _slab_spec(tm, d, nt):
    return pl.BlockSpec((d // LANES, tm, LANES), lambda b, t: (0, b * nt + t, 0))


def _stream_spec(slabs, tm, d, nt):
    return _slab_spec(tm, d, nt) if slabs else _row_spec(tm, d, nt)


def _stream_shape(slabs, m, d):
    return jax.ShapeDtypeStruct((d // LANES, m, LANES) if slabs else (m, d), F32)


def _params():
    return pltpu.CompilerParams(dimension_semantics=("arbitrary", "arbitrary"),
                                vmem_limit_bytes=VMEM_LIMIT_BYTES)


def _ffn_body(x_ref, wpre_ref, wup_ref, cw_ref, cb_ref, wdn_ref, wpost_ref, o_ref,
              carry_ref, act_ref, *stage_refs, tm, dff, fc, slabs_out):
    @pl.when(pl.program_id(1) == 0)
    def _():
        carry_ref[...] = jnp.zeros_like(carry_ref)

    kmax = cw_ref.shape[0] - 1
    x = _load_phased(x_ref, 0, tm)
    h = _rms(x, wpre_ref[...]).astype(BF16)

    def conv_half(col):
        cols = slice(col, col + fc)
        u = jnp.dot(h, wup_ref[:, cols], preferred_element_type=F32)
        s1, s2 = _causal_shifts(u, carry_ref, cols, kmax)
        cw = cw_ref[:, cols]
        return cb_ref[:, cols] + cw[2:3, :] * u + cw[1:2, :] * s1 + cw[0:1, :] * s2

    for c in range(dff // fc):
        g = conv_half(c * fc)
        v = conv_half(dff + c * fc)
        act_ref[:, c * fc:(c + 1) * fc] = (g * _sigmoid(g) * v).astype(BF16)

    f = jnp.dot(act_ref[...], wdn_ref[...], preferred_element_type=F32)
    res = x + _rms(f, wpost_ref[...])
    if slabs_out:
        _store_phased(o_ref, 0, res)
    else:
        _store_phased(stage_refs[0], 0, res)
        _from_slabs(o_ref, stage_refs[0])


def _ffn_layer(xs, batch, wpre, wup, cw, cb, wdn, wpost, slabs_out):
    nl, m, _ = xs.shape
    d = nl * LANES
    seq = m // batch
    tm = min(FFN_TILE, seq)
    nt = seq // tm
    dff = wdn.shape[0]
    kmax = cw.shape[0] - 1
    body = functools.partial(_ffn_body, tm=tm, dff=dff, fc=FFN_COL_BLOCK, slabs_out=slabs_out)
    scratch = [pltpu.VMEM((kmax * CARRY_ROWS, 2 * dff), F32), pltpu.VMEM((tm, dff), BF16)]
    if not slabs_out:
        scratch.append(pltpu.VMEM((nl, tm, LANES), F32))
    return pl.pallas_call(
        body,
        grid=(batch, nt),
        in_specs=[_slab_spec(tm, d, nt), _const_spec((1, d)), _const_spec(wup.shape),
                  _const_spec(cw.shape), _const_spec(cb.shape), _const_spec(wdn.shape),
                  _const_spec((1, d))],
        out_specs=_stream_spec(slabs_out, tm, d, nt),
        out_shape=_stream_shape(slabs_out, m, d),
        scratch_shapes=scratch,
        compiler_params=_params(),
        name="conv_ffn",
    )(xs, wpre, wup, cw, cb, wdn, wpost)


def _pool_body(x_ref, wpre_ref, wpool_ref, scale_ref, wpost_ref, o_ref, carry_ref, *, tm, dg):
    t = pl.program_id(1)

    @pl.when(t == 0)
    def _():
        carry_ref[...] = jnp.zeros_like(carry_ref)

    nl = x_ref.shape[0]
    x = jnp.concatenate([x_ref[k] for k in range(nl)], axis=1)
    h = _rms(x, wpre_ref[...])
    prev = carry_ref[...]
    carry_ref[...] = h[tm - POOL_CARRY_ROWS:, :]
    pos1 = (t * tm + 1 + lax.broadcasted_iota(jnp.int32, (tm, 1), 0)).astype(F32)

    outs = []
    for k, w in enumerate(POOL_WINDOWS):
        hg = h[:, k * dg:(k + 1) * dg]
        s = jnp.concatenate([prev[:, k * dg:(k + 1) * dg], hg], axis=0)
        step = 1
        while step < w:
            s = s + pltpu.roll(s, step, axis=0)
            step *= 2
        mean = s[POOL_CARRY_ROWS:, :] / jnp.minimum(pos1, float(w))
        mixed = (mean - hg).astype(BF16)
        outs.append(jnp.dot(mixed, wpool_ref[k], preferred_element_type=F32))
    out = jnp.concatenate(outs, axis=1) * scale_ref[...]
    res = x + _rms(out, wpost_ref[...])
    for k in range(nl):
        o_ref[k] = res[:, k * LANES:(k + 1) * LANES]


def _pool_layer(xs, batch, wpre, wpool, scale, wpost):
    nl, m, _ = xs.shape
    d = nl * LANES
    seq = m // batch
    tm = min(POOL_TILE, seq)
    nt = seq // tm
    dg = wpool.shape[1]
    body = functools.partial(_pool_body, tm=tm, dg=dg)
    return pl.pallas_call(
        body,
        grid=(batch, nt),
        in_specs=[_slab_spec(tm, d, nt), _const_spec((1, d)), _const_spec(wpool.shape),
                  _const_spec((1, d)), _const_spec((1, d))],
        out_specs=_slab_spec(tm, d, nt),
        out_shape=_stream_shape(True, m, d),
        scratch_shapes=[pltpu.VMEM((POOL_CARRY_ROWS, d), F32)],
        compiler_params=_params(),
        name="pool_mixer",
    )(xs, wpre, wpool, scale, wpost)


def _ssd_body(x_ref, wpre_ref, win_ref, cw_ref, cb_ref, dtb_ref, alog_ref, dexp_ref, nw_ref,
              wout_ref, wpost_ref, o_ref, proj0_ref, proj1_ref, y0_ref, y1_ref, carry_ref,
              state_ref, *, tm, d_inner):
    q = CHUNK
    n = D_STATE
    n_chunks = tm // q
    rq = q // PHASES
    gn = N_SSD_GROUPS * n
    gw = d_inner // N_SSD_GROUPS
    d_xbc = d_inner + 2 * gn
    off_x = d_inner
    off_b = off_x + d_inner
    off_c = off_b + gn
    off_dt = off_c + gn
    d_proj = off_dt + LANES
    d_out = wout_ref.shape[1]
    kmax = cw_ref.shape[0] - 1
    proj_refs = (proj0_ref, proj1_ref)
    y_refs = (y0_ref, y1_ref)

    @pl.when(pl.program_id(1) == 0)
    def _():
        carry_ref[...] = jnp.zeros_like(carry_ref)
        state_ref[...] = jnp.zeros_like(state_ref)

    a_neg = -jnp.exp(alog_ref[...]) * LOG2E
    ri = lax.broadcasted_iota(jnp.int32, (q, q), 0)
    ci = lax.broadcasted_iota(jnp.int32, (q, q), 1)
    t_row = _time_index(ri, rq)
    t_col = _time_index(ci, rq)
    causal = t_row >= t_col
    incl_t = jnp.where(t_row <= t_col, 1.0, 0.0).astype(BF16)
    lane = lax.broadcasted_iota(jnp.int32, (q, LANES), 1)
    left = lane < HEAD_DIM
    last = lane == q - 1


    ctxs = [{} for _ in range(n_chunks)]

    def project(ch):
        proj_ref = proj_refs[ch % 2]
        ctx = ctxs[ch]

        def norm():
            ctx["h"] = _rms(_load_phased(x_ref, ch * q, q), wpre_ref[...]).astype(BF16)

        def block(c0, c1):
            proj_ref[:, c0:c1] = jnp.dot(ctx["h"], win_ref[:, c0:c1], preferred_element_type=F32)

        def decays():
            v = proj_ref[:, off_dt:off_dt + LANES] + dtb_ref[...]
            dt = jnp.maximum(v, 0.0) + jnp.log1p(jnp.exp(-jnp.abs(v)))
            dt_t = dt.T
            a_dt_t = (dt * a_neg).T
            a_cum_t = sum(jnp.dot(p, incl_t, preferred_element_type=F32)
                          for p in _split_bf16(a_dt_t, 3))
            ctx["a_cum"] = a_cum_t.T
            a_last_t = jnp.sum(jnp.where(last, a_cum_t, 0.0), axis=1, keepdims=True)
            ctx["w_end_t"] = dt_t * jnp.exp2(a_last_t - a_cum_t)
            ctx["seg_sub_t"] = a_cum_t - jnp.log2(dt_t)

        return ([norm, functools.partial(block, off_dt, d_proj), decays]
                + [functools.partial(block, c0, min(c0 + PROJ_COL_BLOCK, off_dt))
                   for c0 in range(0, off_dt, PROJ_COL_BLOCK)])

    def conv(ch):
        proj_ref = proj_refs[ch % 2]

        def block(c0):
            cols = slice(c0, c0 + CONV_COL_BLOCK)
            pcols = slice(off_x + c0, off_x + c0 + CONV_COL_BLOCK)
            u = proj_ref[:, pcols]
            s1, s2, s3 = _causal_shifts(u, carry_ref, cols, kmax)
            cw = cw_ref[:, cols]
            y = (cb_ref[:, cols] + cw[3:4, :] * u + cw[2:3, :] * s1 + cw[1:2, :] * s2
                 + cw[0:1, :] * s3)
            proj_ref[:, pcols] = y * _sigmoid(y)

        return [functools.partial(block, c0) for c0 in range(0, d_xbc, CONV_COL_BLOCK)]

    def scan(ch):
        proj_ref = proj_refs[ch % 2]
        y_ref = y_refs[ch % 2]
        ctx = ctxs[ch]

        def group_start(g):
            bt = proj_ref[:, off_b + g * n:off_b + (g + 1) * n].T
            cg = proj_ref[:, off_c + g * n:off_c + (g + 1) * n].astype(BF16)
            ctx["bt"] = bt
            ctx["cb"] = jnp.dot(cg, bt.astype(BF16), preferred_element_type=F32)
            ctx["s_old"] = state_ref[g]
            ctx["y_off"] = jnp.dot(cg, ctx["s_old"].astype(BF16), preferred_element_type=F32)
            ctx["pieces"] = []

        def pair(g, j):
            c0 = g * gw + j * LANES
            ms, bws, e_cols = [], [], []
            for hh in range(2):
                head = c0 // HEAD_DIM + hh
                col = jnp.sum(jnp.where(lane == head, ctx["a_cum"], 0.0), axis=1, keepdims=True)
                seg = col - ctx["seg_sub_t"][head:head + 1, :]
                ms.append(jnp.where(causal, (ctx["cb"] * jnp.exp2(seg)).astype(BF16),
                                    jnp.zeros((), BF16)))
                bws.append((ctx["bt"] * ctx["w_end_t"][head:head + 1, :]).astype(BF16))
                e_cols.append(jnp.exp2(col))
            xp = proj_ref[:, off_x + c0:off_x + c0 + LANES]
            rhs = jnp.concatenate([jnp.where(left, xp, 0.0), jnp.where(left, 0.0, xp)],
                                  axis=0).astype(BF16)
            lhs = jnp.concatenate([jnp.concatenate(ms, axis=1),
                                   jnp.concatenate(bws, axis=1)], axis=0)
            res = jnp.dot(lhs, rhs, preferred_element_type=F32)
            e_pair = jnp.where(left, e_cols[0], e_cols[1])
            y = (res[0:q] + ctx["y_off"][:, j * LANES:(j + 1) * LANES] * e_pair
                 + dexp_ref[:, c0:c0 + LANES] * xp)
            state_ref[g, :, j * LANES:(j + 1) * LANES] = (
                ctx["s_old"][:, j * LANES:(j + 1) * LANES] * e_pair[q - 1:q, :] + res[q:2 * q])
            z = proj_ref[:, c0:c0 + LANES]
            ctx["pieces"].append(y * (z * _sigmoid(z)))

        def group_end(g):
            yg = jnp.concatenate(ctx["pieces"], axis=1)
            yg = yg * lax.rsqrt(jnp.mean(yg * yg, axis=-1, keepdims=True) + EPS)
            y_ref[:, g * gw:(g + 1) * gw] = (yg * nw_ref[:, g * gw:(g + 1) * gw]).astype(BF16)

        pieces = []
        for g in range(N_SSD_GROUPS):
            pieces.append(functools.partial(group_start, g))
            pieces += [functools.partial(pair, g, j) for j in range(gw // LANES)]
            pieces.append(functools.partial(group_end, g))
        return pieces

    def finish(ch):
        y_ref = y_refs[ch % 2]
        outs = []

        def block(c0):
            outs.append(jnp.dot(y_ref[...], wout_ref[:, c0:c0 + OUT_COL_BLOCK],
                                preferred_element_type=F32))

        def store():
            x = _load_phased(x_ref, ch * q, q)
            res = x + _rms(jnp.concatenate(outs, axis=1), wpost_ref[...])
            _store_phased(o_ref, ch * q, res)

        return [functools.partial(block, c0) for c0 in range(0, d_out, OUT_COL_BLOCK)] + [store]

    _run(project(0) + conv(0))
    for ch in range(n_chunks):
        more = ch + 1 < n_chunks
        _run(_spread(scan(ch), project(ch + 1) if more else []))
        _run(_spread(finish(ch), conv(ch + 1) if more else []))


def _ssd_layer(xin, batch, slabs_in, wpre, win, cw, cb, dtb, alog, dexp, nw, wout, wpost):
    if slabs_in:
        nl, m, _ = xin.shape
        d = nl * LANES
    else:
        m, d = xin.shape
        nl = d // LANES
    seq = m // batch
    tm = min(SSD_TILE, seq)
    nt = seq // tm
    d_inner = wout.shape[0]
    d_xbc = cw.shape[1]
    kmax = cw.shape[0] - 1
    body = functools.partial(_ssd_body, tm=tm, d_inner=d_inner)
    scratch = [pltpu.VMEM((CHUNK, win.shape[1]), F32),
               pltpu.VMEM((CHUNK, win.shape[1]), F32),
               pltpu.VMEM((CHUNK, d_inner), BF16),
               pltpu.VMEM((CHUNK, d_inner), BF16),
               pltpu.VMEM((kmax * CARRY_ROWS, d_xbc), F32),
               pltpu.VMEM((N_SSD_GROUPS, D_STATE, d_inner // N_SSD_GROUPS), F32)]
    if not slabs_in:
        body = functools.partial(_ssd_body_natural_in, tm=tm, d_inner=d_inner)
        scratch.append(pltpu.VMEM((nl, tm, LANES), F32))
    return pl.pallas_call(
        body,
        grid=(batch, nt),
        in_specs=[_stream_spec(slabs_in, tm, d, nt), _const_spec((1, d)), _const_spec(win.shape),
                  _const_spec(cw.shape), _const_spec(cb.shape), _const_spec(dtb.shape),
                  _const_spec(alog.shape), _const_spec(dexp.shape), _const_spec(nw.shape),
                  _const_spec(wout.shape), _const_spec((1, d))],
        out_specs=_slab_spec(tm, d, nt),
        out_shape=_stream_shape(True, m, d),
        scratch_shapes=scratch,
        compiler_params=_params(),
        name="ssd_mixer",
    )(xin, wpre, win, cw, cb, dtb, alog, dexp, nw, wout, wpost)


def _ssd_body_natural_in(x_ref, *refs, tm, d_inner):
    stage_ref = refs[-1]
    _to_slabs(stage_ref, x_ref)
    _ssd_body(stage_ref, *refs[:-1], tm=tm, d_inner=d_inner)


def _pad_lanes(v):
    return jnp.pad(v.astype(F32), (0, LANES - v.shape[0]))[None, :]


def kernel(x, ssd_w_in, ssd_conv_w, ssd_conv_b, ssd_dt_bias, ssd_a_log, ssd_d, ssd_norm_w, ssd_w_out, pool_w, pool_scale, ffn_w_up, ffn_conv_w, ffn_conv_b, ffn_w_down, norm_mix_pre, norm_mix_post, norm_ffn_pre, norm_ffn_post):
    batch, seq, d = x.shape
    depth = ffn_w_up.shape[0]
    n_mixers = 2
    n_heads = ssd_dt_bias.shape[1]
    assert seq % CHUNK == 0 and n_heads <= LANES and depth % n_mixers == 0
    xs = x.reshape(batch * seq, d)
    for i in range(depth):
        j = i // n_mixers
        if i % n_mixers == 0:
            win = jnp.pad(ssd_w_in[j], ((0, 0), (0, LANES - n_heads))).astype(BF16)
            xs = _ssd_layer(
                xs, batch, i > 0, norm_mix_pre[i][None, :], win, ssd_conv_w[j],
                ssd_conv_b[j][None, :], _pad_lanes(ssd_dt_bias[j]), _pad_lanes(ssd_a_log[j]),
                jnp.repeat(ssd_d[j].astype(F32), HEAD_DIM)[None, :], ssd_norm_w[j][None, :],
                ssd_w_out[j].astype(BF16), norm_mix_post[i][None, :])
        else:
            xs = _pool_layer(xs, batch, norm_mix_pre[i][None, :], pool_w[j].astype(BF16),
                             pool_scale[j][None, :], norm_mix_post[i][None, :])
        xs = _ffn_layer(xs, batch, norm_ffn_pre[i][None, :], ffn_w_up[i].astype(BF16),
                        ffn_conv_w[i], ffn_conv_b[i][None, :], ffn_w_down[i].astype(BF16),
                        norm_ffn_post[i][None, :], slabs_out=i + 1 < depth)
    return xs.reshape(batch, seq, d)
```

```python
import functools

import jax
import jax.numpy as jnp
from jax import lax
from jax.experimental import pallas as pl
from jax.experimental.pallas import tpu as pltpu

F32 = jnp.float32
BF16 = jnp.bfloat16

EPS = 1e-6
LOG2E = 1.4426950408889634
HEAD_DIM = 64
D_STATE = 128
N_SSD_GROUPS = 4
CHUNK = 128
POOL_WINDOWS = (2, 4, 8, 16)
LANES = 128
CARRY_ROWS = 8
PHASES = 8
PHASE_ROWS = CHUNK // PHASES
VMEM_LIMIT_BYTES = 56 * 1024 * 1024

FFN_TILE = 1024
POOL_TILE = 512
SSD_TILE = 1024
FFN_COL_BLOCK = 256
PROJ_COL_BLOCK = 512
CONV_COL_BLOCK = 256
OUT_COL_BLOCK = 256


def _rms(x, w):
    ms = jnp.mean(x * x, axis=-1, keepdims=True)
    return x * lax.rsqrt(ms + EPS) * w


def _sigmoid(x):
    return 1.0 / (1.0 + jnp.exp2(x * -LOG2E))


def _load_rows(slab_ref, row0, rows):
    return jnp.concatenate([slab_ref[k, row0:row0 + rows, :] for k in range(slab_ref.shape[0])],
                           axis=1)


def _store_rows(slab_ref, row0, val):
    for k in range(slab_ref.shape[0]):
        slab_ref[k, row0:row0 + val.shape[0], :] = val[:, k * LANES:(k + 1) * LANES]


def _to_slabs(slab_ref, x_ref):
    for k in range(slab_ref.shape[0]):
        slab_ref[k] = x_ref[:, k * LANES:(k + 1) * LANES]


def _from_slabs(o_ref, slab_ref):
    for k in range(slab_ref.shape[0]):
        o_ref[:, k * LANES:(k + 1) * LANES] = slab_ref[k]


def _load_phased(slab_ref, row0, rows=CHUNK):
    rq = rows // PHASES
    return jnp.concatenate(
        [jnp.concatenate([slab_ref[k, pl.ds(row0 + p, rq, stride=PHASES), :]
                          for k in range(slab_ref.shape[0])], axis=1)
         for p in range(PHASES)], axis=0)


def _store_phased(slab_ref, row0, val):
    rq = val.shape[0] // PHASES
    for p in range(PHASES):
        for k in range(slab_ref.shape[0]):
            slab_ref[k, pl.ds(row0 + p, rq, stride=PHASES), :] = (
                val[p * rq:(p + 1) * rq, k * LANES:(k + 1) * LANES])


def _time_index(r):
    assert PHASE_ROWS & (PHASE_ROWS - 1) == 0 and CHUNK & (CHUNK - 1) == 0
    r = jnp.bitwise_and(r, CHUNK - 1)
    return (jnp.bitwise_and(r, PHASE_ROWS - 1) * PHASES
            + jnp.right_shift(r, PHASE_ROWS.bit_length() - 1))


def _delays(u, carry_ref, cols, ks):
    n_chunks = u.shape[0] // CHUNK
    rq = PHASE_ROWS
    first = 0 if PHASES in ks else PHASES - max(ks)
    row0 = lax.broadcasted_iota(jnp.int32, (rq, 1), 0) == 0
    shifted = {}
    for c in range(n_chunks):
        for p in range(first, PHASES):
            base = c * CHUNK + p * rq
            if c == 0:
                prev_last = carry_ref[(p + 1) * CARRY_ROWS - 1:(p + 1) * CARRY_ROWS, cols]
            else:
                prev_last = u[base - CHUNK + rq - 1:base - CHUNK + rq, :]
            shifted[c, p] = jnp.where(row0, prev_last, pltpu.roll(u[base:base + rq, :], 1, axis=0))
    for p in range(first, PHASES):
        end = (n_chunks - 1) * CHUNK + (p + 1) * rq
        carry_ref[p * CARRY_ROWS:(p + 1) * CARRY_ROWS, cols] = u[end - CARRY_ROWS:end, :]
    outs = []
    for k in ks:
        pieces = []
        for c in range(n_chunks):
            pieces += [shifted[c, p] for p in range(PHASES - k, PHASES)]
            if k < PHASES:
                pieces.append(u[c * CHUNK:c * CHUNK + (PHASES - k) * rq, :])
        outs.append(jnp.concatenate(pieces, axis=0))
    return outs


def _split_bf16(x, parts):
    out = []
    rem = x
    for _ in range(parts):
        p = rem.astype(BF16)
        out.append(p)
        rem = rem - p.astype(F32)
    return out


def _spread(major, minor):
    out, done = [], 0
    for i, piece in enumerate(major):
        out.append(piece)
        want = (i + 1) * len(minor) // len(major)
        out.extend(minor[done:want])
        done = want
    return out


def _run(pieces):
    for piece in pieces:
        piece()


def _layer_spec(stacked, layer):
    tail = stacked.shape[1:]
    return pl.BlockSpec((None,) + tail, lambda b, t: (layer,) + (0,) * len(tail),
                        pipeline_mode=pl.Buffered(1))


def _row_spec(tm, d, nt):
    return pl.BlockSpec((tm, d), lambda b, t: (b * nt + t, 0))


def _slab_spec(tm, d, nt):
    return pl.BlockSpec((d // LANES, tm, LANES), lambda b, t: (0, b * nt + t, 0))


def _stream_spec(slabs, tm, d, nt):
    return _slab_spec(tm, d, nt) if slabs else _row_spec(tm, d, nt)


def _stream_shape(slabs, m, d):
    return jax.ShapeDtypeStruct((d // LANES, m, LANES) if slabs else (m, d), F32)


def _params():
    return pltpu.CompilerParams(dimension_semantics=("arbitrary", "arbitrary"),
                                vmem_limit_bytes=VMEM_LIMIT_BYTES)


def _ffn_body(x_ref, wpre_ref, wup_ref, cw_ref, cb_ref, wdn_ref, wpost_ref, o_ref,
              carry_ref, act_ref, *stage_refs, tm, dff, fc, natural_out):
    @pl.when(pl.program_id(1) == 0)
    def _():
        carry_ref[...] = jnp.zeros_like(carry_ref)

    x = _load_rows(x_ref, 0, tm)
    h = _rms(x, wpre_ref[...]).astype(BF16)

    def conv_half(col):
        cols = slice(col, col + fc)
        u = jnp.dot(h, wup_ref[:, cols], preferred_element_type=F32)
        d1, d2 = _delays(u, carry_ref, cols, (1, 2))
        cw = cw_ref[:, cols]
        return cb_ref[:, cols] + cw[2:3, :] * u + cw[1:2, :] * d1 + cw[0:1, :] * d2

    for c in range(dff // fc):
        g = conv_half(c * fc)
        v = conv_half(dff + c * fc)
        act_ref[:, c * fc:(c + 1) * fc] = (g * _sigmoid(g) * v).astype(BF16)

    f = jnp.dot(act_ref[...], wdn_ref[...], preferred_element_type=F32)
    res = x + _rms(f, wpost_ref[...])
    if natural_out:
        for r0 in range(0, tm, CHUNK):
            _store_phased(stage_refs[0], r0, res[r0:r0 + CHUNK, :])
        _from_slabs(o_ref, stage_refs[0])
    else:
        _store_rows(o_ref, 0, res)


def _ffn_layer(xs, batch, layer, wpre, wup, cw, cb, wdn, wpost, natural_out):
    nl, m, _ = xs.shape
    d = nl * LANES
    seq = m // batch
    tm = min(FFN_TILE, seq)
    nt = seq // tm
    dff = wdn.shape[1]
    assert cw.shape[1] == 3
    body = functools.partial(_ffn_body, tm=tm, dff=dff, fc=FFN_COL_BLOCK, natural_out=natural_out)
    scratch = [pltpu.VMEM((PHASES * CARRY_ROWS, 2 * dff), F32), pltpu.VMEM((tm, dff), BF16)]
    if natural_out:
        scratch.append(pltpu.VMEM((nl, tm, LANES), F32))
    weights = (wpre, wup, cw, cb, wdn, wpost)
    return pl.pallas_call(
        body,
        grid=(batch, nt),
        in_specs=[_slab_spec(tm, d, nt)] + [_layer_spec(w, layer) for w in weights],
        out_specs=_stream_spec(not natural_out, tm, d, nt),
        out_shape=_stream_shape(not natural_out, m, d),
        scratch_shapes=scratch,
        compiler_params=_params(),
        name="conv_ffn",
    )(xs, *weights)


def _pool_body(x_ref, wpre_ref, wpool_ref, scale_ref, wpost_ref, o_ref, carry_ref, *, tm, dg):
    t = pl.program_id(1)

    @pl.when(t == 0)
    def _():
        carry_ref[...] = jnp.zeros_like(carry_ref)

    x = _load_rows(x_ref, 0, tm)
    h = _rms(x, wpre_ref[...])
    r = lax.broadcasted_iota(jnp.int32, (tm, 1), 0)
    pos1 = (t * tm + jnp.bitwise_and(r, -CHUNK) + _time_index(r) + 1).astype(F32)

    outs = []
    for k, w in enumerate(POOL_WINDOWS):
        cols = slice(k * dg, (k + 1) * dg)
        hg = h[:, cols]
        s, m, level = hg, 1, 0
        while m < w:
            (delayed,) = _delays(s, carry_ref.at[level], cols, (m,))
            s, m, level = s + delayed, 2 * m, level + 1
        mean = s / jnp.minimum(pos1, float(w))
        mixed = (mean - hg).astype(BF16)
        outs.append(jnp.dot(mixed, wpool_ref[k], preferred_element_type=F32))
    out = jnp.concatenate(outs, axis=1) * scale_ref[...]
    _store_rows(o_ref, 0, x + _rms(out, wpost_ref[...]))


def _pool_layer(xs, batch, layer, mixer, wpre, wpool, scale, wpost):
    nl, m, _ = xs.shape
    d = nl * LANES
    seq = m // batch
    tm = min(POOL_TILE, seq)
    nt = seq // tm
    dg = wpool.shape[2]
    assert max(POOL_WINDOWS) <= 2 * PHASES
    levels = max(POOL_WINDOWS).bit_length() - 1
    body = functools.partial(_pool_body, tm=tm, dg=dg)
    return pl.pallas_call(
        body,
        grid=(batch, nt),
        in_specs=[_slab_spec(tm, d, nt), _layer_spec(wpre, layer), _layer_spec(wpool, mixer),
                  _layer_spec(scale, mixer), _layer_spec(wpost, layer)],
        out_specs=_slab_spec(tm, d, nt),
        out_shape=_stream_shape(True, m, d),
        scratch_shapes=[pltpu.VMEM((levels, PHASES * CARRY_ROWS, d), F32)],
        compiler_params=_params(),
        name="pool_mixer",
    )(xs, wpre, wpool, scale, wpost)


def _ssd_body(x_ref, wpre_ref, win_ref, cw_ref, cb_ref, dtb_ref, alog_ref, dexp_ref, nw_ref,
              wout_ref, wpost_ref, o_ref, proj0_ref, proj1_ref, y0_ref, y1_ref, carry_ref,
              state_ref, *stage_refs, tm, d_inner, natural_in):
    q = CHUNK
    n = D_STATE
    n_chunks = tm // q
    gn = N_SSD_GROUPS * n
    gw = d_inner // N_SSD_GROUPS
    d_xbc = d_inner + 2 * gn
    off_x = d_inner
    off_b = off_x + d_inner
    off_c = off_b + gn
    off_dt = off_c + gn
    d_proj = off_dt + LANES
    d_out = wout_ref.shape[1]
    proj_refs = (proj0_ref, proj1_ref)
    y_refs = (y0_ref, y1_ref)

    @pl.when(pl.program_id(1) == 0)
    def _():
        carry_ref[...] = jnp.zeros_like(carry_ref)
        state_ref[...] = jnp.zeros_like(state_ref)

    if natural_in:
        _to_slabs(stage_refs[0], x_ref)
        load_chunk = functools.partial(_load_phased, stage_refs[0])
    else:
        load_chunk = functools.partial(_load_rows, x_ref, rows=q)

    a_neg = -jnp.exp(alog_ref[...]) * LOG2E
    t_row = _time_index(lax.broadcasted_iota(jnp.int32, (q, q), 0))
    t_col = _time_index(lax.broadcasted_iota(jnp.int32, (q, q), 1))
    causal = t_row >= t_col
    incl_t = jnp.where(t_row <= t_col, 1.0, 0.0).astype(BF16)
    lane = lax.broadcasted_iota(jnp.int32, (q, LANES), 1)
    left = lane < HEAD_DIM
    last = lane == q - 1


    ctxs = [{} for _ in range(n_chunks)]

    def project(ch):
        proj_ref = proj_refs[ch % 2]
        ctx = ctxs[ch]

        def norm():
            ctx["h"] = _rms(load_chunk(ch * q), wpre_ref[...]).astype(BF16)

        def block(c0, c1):
            proj_ref[:, c0:c1] = jnp.dot(ctx["h"], win_ref[:, c0:c1], preferred_element_type=F32)

        def decays():
            v = proj_ref[:, off_dt:off_dt + LANES] + dtb_ref[...]
            dt = jnp.maximum(v, 0.0) + jnp.log1p(jnp.exp(-jnp.abs(v)))
            dt_t = dt.T
            a_dt_t = (dt * a_neg).T
            a_cum_t = sum(jnp.dot(p, incl_t, preferred_element_type=F32)
                          for p in _split_bf16(a_dt_t, 3))
            ctx["a_cum"] = a_cum_t.T
            a_last_t = jnp.sum(jnp.where(last, a_cum_t, 0.0), axis=1, keepdims=True)
            ctx["w_end_t"] = dt_t * jnp.exp2(a_last_t - a_cum_t)
            ctx["seg_sub_t"] = a_cum_t - jnp.log2(dt_t)

        return ([norm, functools.partial(block, off_dt, d_proj), decays]
                + [functools.partial(block, c0, min(c0 + PROJ_COL_BLOCK, off_dt))
                   for c0 in range(0, off_dt, PROJ_COL_BLOCK)])

    def conv(ch):
        proj_ref = proj_refs[ch % 2]

        def block(c0):
            cols = slice(c0, c0 + CONV_COL_BLOCK)
            pcols = slice(off_x + c0, off_x + c0 + CONV_COL_BLOCK)
            u = proj_ref[:, pcols]
            d1, d2, d3 = _delays(u, carry_ref, cols, (1, 2, 3))
            cw = cw_ref[:, cols]
            y = (cb_ref[:, cols] + cw[3:4, :] * u + cw[2:3, :] * d1 + cw[1:2, :] * d2
                 + cw[0:1, :] * d3)
            proj_ref[:, pcols] = y * _sigmoid(y)

        return [functools.partial(block, c0) for c0 in range(0, d_xbc, CONV_COL_BLOCK)]

    def scan(ch):
        proj_ref = proj_refs[ch % 2]
        y_ref = y_refs[ch % 2]
        ctx = ctxs[ch]

        def group_start(g):
            bt = proj_ref[:, off_b + g * n:off_b + (g + 1) * n].T
            cg = proj_ref[:, off_c + g * n:off_c + (g + 1) * n].astype(BF16)
            ctx["bt"] = bt
            ctx["cb"] = jnp.dot(cg, bt.astype(BF16), preferred_element_type=F32)
            ctx["s_old"] = state_ref[g]
            ctx["y_off"] = jnp.dot(cg, ctx["s_old"].astype(BF16), preferred_element_type=F32)
            ctx["pieces"] = []

        def pair(g, j):
            c0 = g * gw + j * LANES
            ms, bws, e_cols = [], [], []
            for hh in range(2):
                head = c0 // HEAD_DIM + hh
                col = jnp.sum(jnp.where(lane == head, ctx["a_cum"], 0.0), axis=1, keepdims=True)
                seg = col - ctx["seg_sub_t"][head:head + 1, :]
                ms.append(jnp.where(causal, (ctx["cb"] * jnp.exp2(seg)).astype(BF16),
                                    jnp.zeros((), BF16)))
                bws.append((ctx["bt"] * ctx["w_end_t"][head:head + 1, :]).astype(BF16))
                e_cols.append(jnp.exp2(col))
            xp = proj_ref[:, off_x + c0:off_x + c0 + LANES]
            rhs = jnp.concatenate([jnp.where(left, xp, 0.0), jnp.where(left, 0.0, xp)],
                                  axis=0).astype(BF16)
            lhs = jnp.concatenate([jnp.concatenate(ms, axis=1),
                                   jnp.concatenate(bws, axis=1)], axis=0)
            res = jnp.dot(lhs, rhs, preferred_element_type=F32)
            e_pair = jnp.where(left, e_cols[0], e_cols[1])
            y = (res[0:q] + ctx["y_off"][:, j * LANES:(j + 1) * LANES] * e_pair
                 + dexp_ref[:, c0:c0 + LANES] * xp)
            state_ref[g, :, j * LANES:(j + 1) * LANES] = (
                ctx["s_old"][:, j * LANES:(j + 1) * LANES] * e_pair[q - 1:q, :] + res[q:2 * q])
            z = proj_ref[:, c0:c0 + LANES]
            ctx["pieces"].append(y * (z * _sigmoid(z)))

        def group_end(g):
            yg = jnp.concatenate(ctx["pieces"], axis=1)
            yg = yg * lax.rsqrt(jnp.mean(yg * yg, axis=-1, keepdims=True) + EPS)
            y_ref[:, g * gw:(g + 1) * gw] = (yg * nw_ref[:, g * gw:(g + 1) * gw]).astype(BF16)

        pieces = []
        for g in range(N_SSD_GROUPS):
            pieces.append(functools.partial(group_start, g))
            pieces += [functools.partial(pair, g, j) for j in range(gw // LANES)]
            pieces.append(functools.partial(group_end, g))
        return pieces

    def finish(ch):
        y_ref = y_refs[ch % 2]
        outs = []

        def block(c0):
            outs.append(jnp.dot(y_ref[...], wout_ref[:, c0:c0 + OUT_COL_BLOCK],
                                preferred_element_type=F32))

        def store():
            res = load_chunk(ch * q) + _rms(jnp.concatenate(outs, axis=1), wpost_ref[...])
            _store_rows(o_ref, ch * q, res)

        return [functools.partial(block, c0) for c0 in range(0, d_out, OUT_COL_BLOCK)] + [store]

    _run(project(0) + conv(0))
    for ch in range(n_chunks):
        more = ch + 1 < n_chunks
        _run(_spread(scan(ch), project(ch + 1) if more else []))
        _run(_spread(finish(ch), conv(ch + 1) if more else []))


def _ssd_layer(xin, batch, natural_in, layer, mixer, wpre, win, cw, cb, dtb, alog, dexp, nw, wout,
               wpost):
    if natural_in:
        m, d = xin.shape
        nl = d // LANES
    else:
        nl, m, _ = xin.shape
        d = nl * LANES
    seq = m // batch
    tm = min(SSD_TILE, seq)
    nt = seq // tm
    d_inner = wout.shape[1]
    d_proj = win.shape[2]
    d_xbc = cw.shape[2]
    assert cw.shape[1] == 4
    body = functools.partial(_ssd_body, tm=tm, d_inner=d_inner, natural_in=natural_in)
    scratch = [pltpu.VMEM((CHUNK, d_proj), F32),
               pltpu.VMEM((CHUNK, d_proj), F32),
               pltpu.VMEM((CHUNK, d_inner), BF16),
               pltpu.VMEM((CHUNK, d_inner), BF16),
               pltpu.VMEM((PHASES * CARRY_ROWS, d_xbc), F32),
               pltpu.VMEM((N_SSD_GROUPS, D_STATE, d_inner // N_SSD_GROUPS), F32)]
    if natural_in:
        scratch.append(pltpu.VMEM((nl, tm, LANES), F32))
    mixer_params = (win, cw, cb, dtb, alog, dexp, nw, wout)
    return pl.pallas_call(
        body,
        grid=(batch, nt),
        in_specs=([_stream_spec(not natural_in, tm, d, nt), _layer_spec(wpre, layer)]
                  + [_layer_spec(w, mixer) for w in mixer_params] + [_layer_spec(wpost, layer)]),
        out_specs=_slab_spec(tm, d, nt),
        out_shape=_stream_shape(True, m, d),
        scratch_shapes=scratch,
        compiler_params=_params(),
        name="ssd_mixer",
    )(xin, wpre, *mixer_params, wpost)


def _rows(v):
    return v.astype(F32)[:, None, :]


def kernel(x, ssd_w_in, ssd_conv_w, ssd_conv_b, ssd_dt_bias, ssd_a_log, ssd_d, ssd_norm_w, ssd_w_out, pool_w, pool_scale, ffn_w_up, ffn_conv_w, ffn_conv_b, ffn_w_down, norm_mix_pre, norm_mix_post, norm_ffn_pre, norm_ffn_post):
    batch, seq, d = x.shape
    depth = ffn_w_up.shape[0]
    n_mixers = 2
    n_heads = ssd_dt_bias.shape[1]
    assert seq % CHUNK == 0 and n_heads <= LANES and depth % n_mixers == 0
    head_pad = ((0, 0), (0, LANES - n_heads))
    ssd_params = (jnp.pad(ssd_w_in.astype(BF16), ((0, 0),) + head_pad), ssd_conv_w, _rows(ssd_conv_b),
                  _rows(jnp.pad(ssd_dt_bias, head_pad)), _rows(jnp.pad(ssd_a_log, head_pad)),
                  _rows(jnp.repeat(ssd_d, HEAD_DIM, axis=1)), _rows(ssd_norm_w), ssd_w_out.astype(BF16))
    pool_params = (pool_w.astype(BF16), _rows(pool_scale))
    ffn_params = (ffn_w_up.astype(BF16), ffn_conv_w, _rows(ffn_conv_b), ffn_w_down.astype(BF16))
    mix_pre, mix_post = _rows(norm_mix_pre), _rows(norm_mix_post)
    ffn_pre, ffn_post = _rows(norm_ffn_pre), _rows(norm_ffn_post)
    xs = x.reshape(batch * seq, d)
    for i in range(depth):
        j = i // n_mixers
        if i % n_mixers == 0:
            xs = _ssd_layer(xs, batch, i == 0, i, j, mix_pre, *ssd_params, mix_post)
        else:
            xs = _pool_layer(xs, batch, i, j, mix_pre, *pool_params, mix_post)
        xs = _ffn_layer(xs, batch, i, ffn_pre, *ffn_params, ffn_post, natural_out=i + 1 == depth)
    return xs.reshape(batch, seq, d)
```

```python
import functools

import jax
import jax.numpy as jnp
from jax import lax
from jax.experimental import pallas as pl
from jax.experimental.pallas import tpu as pltpu

F32 = jnp.float32
BF16 = jnp.bfloat16

EPS = 1e-6
LOG2E = 1.4426950408889634
HEAD_DIM = 64
D_STATE = 128
N_SSD_GROUPS = 4
CHUNK = 128
POOL_WINDOWS = (2, 4, 8, 16)
LANES = 128
CARRY_ROWS = 8
PHASES = 8
PHASE_ROWS = CHUNK // PHASES
VMEM_LIMIT_BYTES = 56 * 1024 * 1024

FFN_TILE = 1024
FFN_DOWN_BLOCKS = 4
POOL_TILE = 1024
SSD_TILE = 1024
FFN_COL_BLOCK = 256
PROJ_COL_BLOCK = 256
CONV_COL_BLOCK = 256
OUT_COL_BLOCK = 256


def _rms(x, w):
    ms = jnp.mean(x * x, axis=-1, keepdims=True)
    return x * lax.rsqrt(ms + EPS) * w


def _sigmoid(x):
    return 1.0 / (1.0 + jnp.exp2(x * -LOG2E))


def _load_rows(slab_ref, row0, rows):
    return jnp.concatenate([slab_ref[k, row0:row0 + rows, :] for k in range(slab_ref.shape[0])],
                           axis=1)


def _store_rows(slab_ref, row0, val):
    for k in range(slab_ref.shape[0]):
        slab_ref[k, row0:row0 + val.shape[0], :] = val[:, k * LANES:(k + 1) * LANES]


def _from_slabs(o_ref, slab_ref):
    for k in range(slab_ref.shape[0]):
        o_ref[:, k * LANES:(k + 1) * LANES] = slab_ref[k]


def _load_phased(slab_ref, row0, rows=CHUNK):
    rq = rows // PHASES
    return jnp.concatenate(
        [jnp.concatenate([slab_ref[k, pl.ds(row0 + p, rq, stride=PHASES), :]
                          for k in range(slab_ref.shape[0])], axis=1)
         for p in range(PHASES)], axis=0)


def _store_phased(slab_ref, row0, val):
    rq = val.shape[0] // PHASES
    for p in range(PHASES):
        for k in range(slab_ref.shape[0]):
            slab_ref[k, pl.ds(row0 + p, rq, stride=PHASES), :] = (
                val[p * rq:(p + 1) * rq, k * LANES:(k + 1) * LANES])


def _time_index(r):
    assert PHASE_ROWS & (PHASE_ROWS - 1) == 0 and CHUNK & (CHUNK - 1) == 0
    r = jnp.bitwise_and(r, CHUNK - 1)
    return (jnp.bitwise_and(r, PHASE_ROWS - 1) * PHASES
            + jnp.right_shift(r, PHASE_ROWS.bit_length() - 1))


def _delays(u, carry_ref, cols, ks):
    n_chunks = u.shape[0] // CHUNK
    rq = PHASE_ROWS
    first = 0 if PHASES in ks else PHASES - max(ks)
    row0 = lax.broadcasted_iota(jnp.int32, (rq, 1), 0) == 0
    shifted = {}
    for c in range(n_chunks):
        for p in range(first, PHASES):
            base = c * CHUNK + p * rq
            if c == 0:
                prev_last = carry_ref[(p + 1) * CARRY_ROWS - 1:(p + 1) * CARRY_ROWS, cols]
            else:
                prev_last = u[base - CHUNK + rq - 1:base - CHUNK + rq, :]
            shifted[c, p] = jnp.where(row0, prev_last, pltpu.roll(u[base:base + rq, :], 1, axis=0))
    for p in range(first, PHASES):
        end = (n_chunks - 1) * CHUNK + (p + 1) * rq
        carry_ref[p * CARRY_ROWS:(p + 1) * CARRY_ROWS, cols] = u[end - CARRY_ROWS:end, :]
    outs = []
    for k in ks:
        pieces = []
        for c in range(n_chunks):
            pieces += [shifted[c, p] for p in range(PHASES - k, PHASES)]
            if k < PHASES:
                pieces.append(u[c * CHUNK:c * CHUNK + (PHASES - k) * rq, :])
        outs.append(jnp.concatenate(pieces, axis=0))
    return outs


def _split_bf16(x, parts):
    out = []
    rem = x
    for _ in range(parts):
        p = rem.astype(BF16)
        out.append(p)
        rem = rem - p.astype(F32)
    return out


def _spread(major, minor):
    out, done = [], 0
    for i, piece in enumerate(major):
        out.append(piece)
        want = (i + 1) * len(minor) // len(major)
        out.extend(minor[done:want])
        done = want
    return out


def _run(pieces):
    for piece in pieces:
        piece()


def _layer_spec(stacked, layer):
    tail = stacked.shape[1:]
    return pl.BlockSpec((None,) + tail, lambda b, t: (layer,) + (0,) * len(tail),
                        pipeline_mode=pl.Buffered(1))


def _row_spec(tm, d, nt):
    return pl.BlockSpec((tm, d), lambda b, t: (b * nt + t, 0))


def _slab_spec(tm, d, nt):
    return pl.BlockSpec((d // LANES, tm, LANES), lambda b, t: (0, b * nt + t, 0))


def _stream_spec(slabs, tm, d, nt):
    return _slab_spec(tm, d, nt) if slabs else _row_spec(tm, d, nt)


def _stream_shape(slabs, m, d):
    return jax.ShapeDtypeStruct((d // LANES, m, LANES) if slabs else (m, d), F32)


def _params():
    return pltpu.CompilerParams(dimension_semantics=("arbitrary", "arbitrary"),
                                vmem_limit_bytes=VMEM_LIMIT_BYTES)


def _ffn_body(x_ref, wpre_ref, wup_ref, cw_ref, cb_ref, wdn_ref, wpost_ref, o_ref,
              carry_ref, act_ref, *stage_refs, tm, dff, fc, natural_out):
    @pl.when(pl.program_id(1) == 0)
    def _():
        carry_ref[...] = jnp.zeros_like(carry_ref)

    x = _load_rows(x_ref, 0, tm)
    h = _rms(x, wpre_ref[...]).astype(BF16)

    def conv_half(col):
        cols = slice(col, col + fc)
        u = jnp.dot(h, wup_ref[:, cols], preferred_element_type=F32)
        d1, d2 = _delays(u, carry_ref, cols, (1, 2))
        cw = cw_ref[:, cols]
        return cb_ref[:, cols] + cw[2:3, :] * u + cw[1:2, :] * d1 + cw[0:1, :] * d2

    for c in range(dff // fc):
        g = conv_half(c * fc)
        v = conv_half(dff + c * fc)
        act_ref[:, c * fc:(c + 1) * fc] = (g * _sigmoid(g) * v).astype(BF16)

    rb = tm // FFN_DOWN_BLOCKS
    fs = [jnp.dot(act_ref[i * rb:(i + 1) * rb, :], wdn_ref[...], preferred_element_type=F32)
          for i in range(FFN_DOWN_BLOCKS)]
    for i, f in enumerate(fs):
        res = x[i * rb:(i + 1) * rb, :] + _rms(f, wpost_ref[...])
        if natural_out:
            for r0 in range(0, rb, CHUNK):
                _store_phased(stage_refs[0], i * rb + r0, res[r0:r0 + CHUNK, :])
        else:
            _store_rows(o_ref, i * rb, res)
    if natural_out:
        _from_slabs(o_ref, stage_refs[0])


def _ffn_layer(xs, batch, layer, wpre, wup, cw, cb, wdn, wpost, natural_out):
    nl, m, _ = xs.shape
    d = nl * LANES
    seq = m // batch
    tm = min(FFN_TILE, seq)
    nt = seq // tm
    dff = wdn.shape[1]
    assert cw.shape[1] == 3
    body = functools.partial(_ffn_body, tm=tm, dff=dff, fc=FFN_COL_BLOCK, natural_out=natural_out)
    scratch = [pltpu.VMEM((PHASES * CARRY_ROWS, 2 * dff), F32), pltpu.VMEM((tm, dff), BF16)]
    if natural_out:
        scratch.append(pltpu.VMEM((nl, tm, LANES), F32))
    weights = (wpre, wup, cw, cb, wdn, wpost)
    return pl.pallas_call(
        body,
        grid=(batch, nt),
        in_specs=[_slab_spec(tm, d, nt)] + [_layer_spec(w, layer) for w in weights],
        out_specs=_stream_spec(not natural_out, tm, d, nt),
        out_shape=_stream_shape(not natural_out, m, d),
        scratch_shapes=scratch,
        compiler_params=_params(),
        name="conv_ffn",
    )(xs, *weights)


def _pool_body(x_ref, wpre_ref, wpool_ref, scale_ref, wpost_ref, o_ref, carry_ref, *, tm, dg):
    t = pl.program_id(1)

    @pl.when(t == 0)
    def _():
        carry_ref[...] = jnp.zeros_like(carry_ref)

    x = _load_rows(x_ref, 0, tm)
    h = _rms(x, wpre_ref[...])
    r = lax.broadcasted_iota(jnp.int32, (tm, 1), 0)
    pos1 = (t * tm + jnp.bitwise_and(r, -CHUNK) + _time_index(r) + 1).astype(F32)

    outs = []
    for k, w in enumerate(POOL_WINDOWS):
        cols = slice(k * dg, (k + 1) * dg)
        hg = h[:, cols]
        s, m, level = hg, 1, 0
        while m < w:
            (delayed,) = _delays(s, carry_ref.at[level], cols, (m,))
            s, m, level = s + delayed, 2 * m, level + 1
        mean = s / jnp.minimum(pos1, float(w))
        mixed = (mean - hg).astype(BF16)
        outs.append(jnp.dot(mixed, wpool_ref[k], preferred_element_type=F32))
    out = jnp.concatenate(outs, axis=1) * scale_ref[...]
    _store_rows(o_ref, 0, x + _rms(out, wpost_ref[...]))


def _pool_layer(xs, batch, layer, mixer, wpre, wpool, scale, wpost):
    nl, m, _ = xs.shape
    d = nl * LANES
    seq = m // batch
    tm = min(POOL_TILE, seq)
    nt = seq // tm
    dg = wpool.shape[2]
    assert max(POOL_WINDOWS) <= 2 * PHASES
    levels = max(POOL_WINDOWS).bit_length() - 1
    body = functools.partial(_pool_body, tm=tm, dg=dg)
    return pl.pallas_call(
        body,
        grid=(batch, nt),
        in_specs=[_slab_spec(tm, d, nt), _layer_spec(wpre, layer), _layer_spec(wpool, mixer),
                  _layer_spec(scale, mixer), _layer_spec(wpost, layer)],
        out_specs=_slab_spec(tm, d, nt),
        out_shape=_stream_shape(True, m, d),
        scratch_shapes=[pltpu.VMEM((levels, PHASES * CARRY_ROWS, d), F32)],
        compiler_params=_params(),
        name="pool_mixer",
    )(xs, wpre, wpool, scale, wpost)


def _ssd_body(x_ref, wpre_ref, win_ref, cw_ref, cb_ref, dtb_ref, alog_ref, dexp_ref, nw_ref,
              wout_ref, wpost_ref, o_ref, proj0_ref, proj1_ref, y0_ref, y1_ref, carry_ref,
              state_ref, *stage_refs, tm, d_inner, natural_in):
    q = CHUNK
    n = D_STATE
    n_chunks = tm // q
    gn = N_SSD_GROUPS * n
    gw = d_inner // N_SSD_GROUPS
    d_xbc = d_inner + 2 * gn
    off_x = d_inner
    off_b = off_x + d_inner
    off_c = off_b + gn
    off_dt = off_c + gn
    d_proj = off_dt + LANES
    d_out = wout_ref.shape[1]
    proj_refs = (proj0_ref, proj1_ref)
    y_refs = (y0_ref, y1_ref)

    @pl.when(pl.program_id(1) == 0)
    def _():
        carry_ref[...] = jnp.zeros_like(carry_ref)
        state_ref[...] = jnp.zeros_like(state_ref)

    if natural_in:
        load_chunk = functools.partial(_load_phased, stage_refs[0])
    else:
        load_chunk = functools.partial(_load_rows, x_ref, rows=q)

    a_neg = -jnp.exp(alog_ref[...]) * LOG2E
    t_row = _time_index(lax.broadcasted_iota(jnp.int32, (q, q), 0))
    t_col = _time_index(lax.broadcasted_iota(jnp.int32, (q, q), 1))
    causal = t_row >= t_col
    incl_t = jnp.where(t_row <= t_col, 1.0, 0.0).astype(BF16)
    lane = lax.broadcasted_iota(jnp.int32, (q, LANES), 1)
    left = lane < HEAD_DIM
    last = lane == q - 1


    ctxs = [{} for _ in range(n_chunks)]

    def project(ch):
        proj_ref = proj_refs[ch % 2]
        ctx = ctxs[ch]

        def norm():
            if natural_in:
                for k in range(stage_refs[0].shape[0]):
                    stage_refs[0][k, ch * q:(ch + 1) * q, :] = (
                        x_ref[ch * q:(ch + 1) * q, k * LANES:(k + 1) * LANES])
            ctx["h"] = _rms(load_chunk(ch * q), wpre_ref[...]).astype(BF16)

        def block(c0, c1):
            proj_ref[:, c0:c1] = jnp.dot(ctx["h"], win_ref[:, c0:c1], preferred_element_type=F32)

        def decays():
            v = proj_ref[:, off_dt:off_dt + LANES] + dtb_ref[...]
            dt = jnp.maximum(v, 0.0) + jnp.log1p(jnp.exp(-jnp.abs(v)))
            dt_t = dt.T
            a_dt_t = (dt * a_neg).T
            a_cum_t = sum(jnp.dot(p, incl_t, preferred_element_type=F32)
                          for p in _split_bf16(a_dt_t, 3))
            ctx["a_cum"] = a_cum_t.T
            a_last_t = jnp.sum(jnp.where(last, a_cum_t, 0.0), axis=1, keepdims=True)
            ctx["w_end_t"] = dt_t * jnp.exp2(a_last_t - a_cum_t)
            ctx["seg_sub_t"] = a_cum_t - jnp.log2(dt_t)

        return ([norm, functools.partial(block, off_dt, d_proj), decays]
                + [functools.partial(block, c0, min(c0 + PROJ_COL_BLOCK, off_dt))
                   for c0 in range(0, off_dt, PROJ_COL_BLOCK)])

    def conv(ch):
        proj_ref = proj_refs[ch % 2]

        def block(c0):
            cols = slice(c0, c0 + CONV_COL_BLOCK)
            pcols = slice(off_x + c0, off_x + c0 + CONV_COL_BLOCK)
            u = proj_ref[:, pcols]
            d1, d2, d3 = _delays(u, carry_ref, cols, (1, 2, 3))
            cw = cw_ref[:, cols]
            y = (cb_ref[:, cols] + cw[3:4, :] * u + cw[2:3, :] * d1 + cw[1:2, :] * d2
                 + cw[0:1, :] * d3)
            proj_ref[:, pcols] = y * _sigmoid(y)

        return [functools.partial(block, c0) for c0 in range(0, d_xbc, CONV_COL_BLOCK)]

    def scan(ch):
        proj_ref = proj_refs[ch % 2]
        y_ref = y_refs[ch % 2]
        ctx = ctxs[ch]
        groups = [{} for _ in range(N_SSD_GROUPS)]

        def group_start(g):
            bt = proj_ref[:, off_b + g * n:off_b + (g + 1) * n].T
            cg = proj_ref[:, off_c + g * n:off_c + (g + 1) * n].astype(BF16)
            gctx = groups[g]
            gctx["bt"] = bt
            gctx["cb"] = jnp.dot(cg, bt.astype(BF16), preferred_element_type=F32)
            gctx["s_old"] = state_ref[g]
            gctx["y_off"] = jnp.dot(cg, gctx["s_old"].astype(BF16), preferred_element_type=F32)
            gctx["pieces"] = []

        def pair(g, j):
            gctx = groups[g]
            c0 = g * gw + j * LANES
            ms, bws, e_cols = [], [], []
            for hh in range(2):
                head = c0 // HEAD_DIM + hh
                col = jnp.sum(jnp.where(lane == head, ctx["a_cum"], 0.0), axis=1, keepdims=True)
                seg = col - ctx["seg_sub_t"][head:head + 1, :]
                ms.append(jnp.where(causal, (gctx["cb"] * jnp.exp2(seg)).astype(BF16),
                                    jnp.zeros((), BF16)))
                bws.append((gctx["bt"] * ctx["w_end_t"][head:head + 1, :]).astype(BF16))
                e_cols.append(jnp.exp2(col))
            xp = proj_ref[:, off_x + c0:off_x + c0 + LANES]
            rhs = jnp.concatenate([jnp.where(left, xp, 0.0), jnp.where(left, 0.0, xp)],
                                  axis=0).astype(BF16)
            lhs = jnp.concatenate([jnp.concatenate(ms, axis=1),
                                   jnp.concatenate(bws, axis=1)], axis=0)
            res = jnp.dot(lhs, rhs, preferred_element_type=F32)
            e_pair = jnp.where(left, e_cols[0], e_cols[1])
            y = (res[0:q] + gctx["y_off"][:, j * LANES:(j + 1) * LANES] * e_pair
                 + dexp_ref[:, c0:c0 + LANES] * xp)
            state_ref[g, :, j * LANES:(j + 1) * LANES] = (
                gctx["s_old"][:, j * LANES:(j + 1) * LANES] * e_pair[q - 1:q, :] + res[q:2 * q])
            z = proj_ref[:, c0:c0 + LANES]
            gctx["pieces"].append(y * (z * _sigmoid(z)))

        def group_end(g):
            yg = jnp.concatenate(groups[g]["pieces"], axis=1)
            yg = yg * lax.rsqrt(jnp.mean(yg * yg, axis=-1, keepdims=True) + EPS)
            y_ref[:, g * gw:(g + 1) * gw] = (yg * nw_ref[:, g * gw:(g + 1) * gw]).astype(BF16)

        pieces = [functools.partial(group_start, 0)]
        for g in range(N_SSD_GROUPS):
            if g + 1 < N_SSD_GROUPS:
                pieces.append(functools.partial(group_start, g + 1))
            pieces += [functools.partial(pair, g, j) for j in range(gw // LANES)]
            pieces.append(functools.partial(group_end, g))
        return pieces

    def finish(ch):
        y_ref = y_refs[ch % 2]
        outs = []

        def block(c0):
            outs.append(jnp.dot(y_ref[...], wout_ref[:, c0:c0 + OUT_COL_BLOCK],
                                preferred_element_type=F32))

        def store():
            res = load_chunk(ch * q) + _rms(jnp.concatenate(outs, axis=1), wpost_ref[...])
            _store_rows(o_ref, ch * q, res)

        return [functools.partial(block, c0) for c0 in range(0, d_out, OUT_COL_BLOCK)] + [store]

    _run(project(0) + conv(0))
    for ch in range(n_chunks):
        more = ch + 1 < n_chunks
        _run(_spread(scan(ch), project(ch + 1) if more else []))
        _run(_spread(finish(ch), conv(ch + 1) if more else []))


def _ssd_layer(xin, batch, natural_in, layer, mixer, wpre, win, cw, cb, dtb, alog, dexp, nw, wout,
               wpost):
    if natural_in:
        m, d = xin.shape
        nl = d // LANES
    else:
        nl, m, _ = xin.shape
        d = nl * LANES
    seq = m // batch
    tm = min(SSD_TILE, seq)
    nt = seq // tm
    d_inner = wout.shape[1]
    d_proj = win.shape[2]
    d_xbc = cw.shape[2]
    assert cw.shape[1] == 4
    body = functools.partial(_ssd_body, tm=tm, d_inner=d_inner, natural_in=natural_in)
    scratch = [pltpu.VMEM((CHUNK, d_proj), F32),
               pltpu.VMEM((CHUNK, d_proj), F32),
               pltpu.VMEM((CHUNK, d_inner), BF16),
               pltpu.VMEM((CHUNK, d_inner), BF16),
               pltpu.VMEM((PHASES * CARRY_ROWS, d_xbc), F32),
               pltpu.VMEM((N_SSD_GROUPS, D_STATE, d_inner // N_SSD_GROUPS), F32)]
    if natural_in:
        scratch.append(pltpu.VMEM((nl, tm, LANES), F32))
    mixer_params = (win, cw, cb, dtb, alog, dexp, nw, wout)
    return pl.pallas_call(
        body,
        grid=(batch, nt),
        in_specs=([_stream_spec(not natural_in, tm, d, nt), _layer_spec(wpre, layer)]
                  + [_layer_spec(w, mixer) for w in mixer_params] + [_layer_spec(wpost, layer)]),
        out_specs=_slab_spec(tm, d, nt),
        out_shape=_stream_shape(True, m, d),
        scratch_shapes=scratch,
        compiler_params=_params(),
        name="ssd_mixer",
    )(xin, wpre, *mixer_params, wpost)


def _rows(v):
    return v.astype(F32)[:, None, :]


def kernel(x, ssd_w_in, ssd_conv_w, ssd_conv_b, ssd_dt_bias, ssd_a_log, ssd_d, ssd_norm_w, ssd_w_out, pool_w, pool_scale, ffn_w_up, ffn_conv_w, ffn_conv_b, ffn_w_down, norm_mix_pre, norm_mix_post, norm_ffn_pre, norm_ffn_post):
    batch, seq, d = x.shape
    depth = ffn_w_up.shape[0]
    n_mixers = 2
    n_heads = ssd_dt_bias.shape[1]
    assert seq % CHUNK == 0 and n_heads <= LANES and depth % n_mixers == 0
    head_pad = ((0, 0), (0, LANES - n_heads))
    ssd_params = (jnp.pad(ssd_w_in.astype(BF16), ((0, 0),) + head_pad), ssd_conv_w, _rows(ssd_conv_b),
                  _rows(jnp.pad(ssd_dt_bias, head_pad)), _rows(jnp.pad(ssd_a_log, head_pad)),
                  _rows(jnp.repeat(ssd_d, HEAD_DIM, axis=1)), _rows(ssd_norm_w), ssd_w_out.astype(BF16))
    pool_params = (pool_w.astype(BF16), _rows(pool_scale))
    ffn_params = (ffn_w_up.astype(BF16), ffn_conv_w, _rows(ffn_conv_b), ffn_w_down.astype(BF16))
    mix_pre, mix_post = _rows(norm_mix_pre), _rows(norm_mix_post)
    ffn_pre, ffn_post = _rows(norm_ffn_pre), _rows(norm_ffn_post)
    xs = x.reshape(batch * seq, d)
    for i in range(depth):
        j = i // n_mixers
        if i % n_mixers == 0:
            xs = _ssd_layer(xs, batch, i == 0, i, j, mix_pre, *ssd_params, mix_post)
        else:
            xs = _pool_layer(xs, batch, i, j, mix_pre, *pool_params, mix_post)
        xs = _ffn_layer(xs, batch, i, ffn_pre, *ffn_params, ffn_post, natural_out=i + 1 == depth)
    return xs.reshape(batch, seq, d)
```

```python
import functools

import jax
import jax.numpy as jnp
from jax import lax
from jax.experimental import pallas as pl
from jax.experimental.pallas import tpu as pltpu

F32 = jnp.float32
BF16 = jnp.bfloat16

EPS = 1e-6
LOG2E = 1.4426950408889634
HEAD_DIM = 64
D_STATE = 128
N_SSD_GROUPS = 4
CHUNK = 128
POOL_WINDOWS = (2, 4, 8, 16)
LANES = 128
CARRY_ROWS = 8
PHASES = 8
PHASE_ROWS = CHUNK // PHASES
VMEM_LIMIT_BYTES = 56 * 1024 * 1024

FFN_TILE = 1024
FFN_DOWN_BLOCKS = 4
POOL_TILE = 1024
SSD_TILE = 1024
FFN_COL_BLOCK = 256
PROJ_COL_BLOCK = 256
CONV_COL_BLOCK = 256
OUT_COL_BLOCK = 256


def _rms(x, w):
    ms = jnp.mean(x * x, axis=-1, keepdims=True)
    return x * lax.rsqrt(ms + EPS) * w


def _sigmoid(x):
    return 1.0 / (1.0 + jnp.exp2(x * -LOG2E))


def _load_rows(slab_ref, row0, rows):
    return jnp.concatenate([slab_ref[k, row0:row0 + rows, :] for k in range(slab_ref.shape[0])],
                           axis=1)


def _store_rows(slab_ref, row0, val):
    for k in range(slab_ref.shape[0]):
        slab_ref[k, row0:row0 + val.shape[0], :] = val[:, k * LANES:(k + 1) * LANES]


def _from_slabs(o_ref, slab_ref):
    for k in range(slab_ref.shape[0]):
        o_ref[:, k * LANES:(k + 1) * LANES] = slab_ref[k]


def _load_phased(slab_ref, row0, rows=CHUNK):
    rq = rows // PHASES
    return jnp.concatenate(
        [jnp.concatenate([slab_ref[k, pl.ds(row0 + p, rq, stride=PHASES), :]
                          for k in range(slab_ref.shape[0])], axis=1)
         for p in range(PHASES)], axis=0)


def _store_phased(slab_ref, row0, val):
    rq = val.shape[0] // PHASES
    for p in range(PHASES):
        for k in range(slab_ref.shape[0]):
            slab_ref[k, pl.ds(row0 + p, rq, stride=PHASES), :] = (
                val[p * rq:(p + 1) * rq, k * LANES:(k + 1) * LANES])


def _time_index(r):
    assert PHASE_ROWS & (PHASE_ROWS - 1) == 0 and CHUNK & (CHUNK - 1) == 0
    r = jnp.bitwise_and(r, CHUNK - 1)
    return (jnp.bitwise_and(r, PHASE_ROWS - 1) * PHASES
            + jnp.right_shift(r, PHASE_ROWS.bit_length() - 1))


def _delays(u, carry_ref, cols, ks):
    n_chunks = u.shape[0] // CHUNK
    rq = PHASE_ROWS
    first = 0 if PHASES in ks else PHASES - max(ks)
    row0 = lax.broadcasted_iota(jnp.int32, (rq, 1), 0) == 0
    shifted = {}
    for c in range(n_chunks):
        for p in range(first, PHASES):
            base = c * CHUNK + p * rq
            if c == 0:
                prev_last = carry_ref[(p + 1) * CARRY_ROWS - 1:(p + 1) * CARRY_ROWS, cols]
            else:
                prev_last = u[base - CHUNK + rq - 1:base - CHUNK + rq, :]
            shifted[c, p] = jnp.where(row0, prev_last, pltpu.roll(u[base:base + rq, :], 1, axis=0))
    for p in range(first, PHASES):
        end = (n_chunks - 1) * CHUNK + (p + 1) * rq
        carry_ref[p * CARRY_ROWS:(p + 1) * CARRY_ROWS, cols] = u[end - CARRY_ROWS:end, :]
    outs = []
    for k in ks:
        pieces = []
        for c in range(n_chunks):
            pieces += [shifted[c, p] for p in range(PHASES - k, PHASES)]
            if k < PHASES:
                pieces.append(u[c * CHUNK:c * CHUNK + (PHASES - k) * rq, :])
        outs.append(jnp.concatenate(pieces, axis=0))
    return outs


def _split_bf16(x, parts):
    out = []
    rem = x
    for _ in range(parts):
        p = rem.astype(BF16)
        out.append(p)
        rem = rem - p.astype(F32)
    return out


def _spread(major, minor):
    out, done = [], 0
    for i, piece in enumerate(major):
        out.append(piece)
        want = (i + 1) * len(minor) // len(major)
        out.extend(minor[done:want])
        done = want
    return out


def _run(pieces):
    for piece in pieces:
        piece()


def _layer_spec(stacked, layer):
    tail = stacked.shape[1:]
    return pl.BlockSpec((None,) + tail, lambda b, t: (layer,) + (0,) * len(tail),
                        pipeline_mode=pl.Buffered(1))


def _row_spec(tm, d, nt):
    return pl.BlockSpec((tm, d), lambda b, t: (b * nt + t, 0))


def _slab_spec(tm, d, nt):
    return pl.BlockSpec((d // LANES, tm, LANES), lambda b, t: (0, b * nt + t, 0))


def _stream_spec(slabs, tm, d, nt):
    return _slab_spec(tm, d, nt) if slabs else _row_spec(tm, d, nt)


def _stream_shape(slabs, m, d):
    return jax.ShapeDtypeStruct((d // LANES, m, LANES) if slabs else (m, d), F32)


def _params():
    return pltpu.CompilerParams(dimension_semantics=("arbitrary", "arbitrary"),
                                vmem_limit_bytes=VMEM_LIMIT_BYTES)


def _ffn_body(x_ref, wpre_ref, wup_ref, cw_ref, cb_ref, wdn_ref, wpost_ref, o_ref,
              carry_ref, act_ref, *stage_refs, tm, dff, fc, natural_out):
    @pl.when(pl.program_id(1) == 0)
    def _():
        carry_ref[...] = jnp.zeros_like(carry_ref)

    x = _load_rows(x_ref, 0, tm)
    h = _rms(x, wpre_ref[...]).astype(BF16)

    def conv_half(col):
        cols = slice(col, col + fc)
        u = jnp.dot(h, wup_ref[:, cols], preferred_element_type=F32)
        d1, d2 = _delays(u, carry_ref, cols, (1, 2))
        cw = cw_ref[:, cols]
        return cb_ref[:, cols] + cw[2:3, :] * u + cw[1:2, :] * d1 + cw[0:1, :] * d2

    for c in range(dff // fc):
        g = conv_half(c * fc)
        v = conv_half(dff + c * fc)
        act_ref[:, c * fc:(c + 1) * fc] = (g * _sigmoid(g) * v).astype(BF16)

    rb = tm // FFN_DOWN_BLOCKS
    fs = [jnp.dot(act_ref[i * rb:(i + 1) * rb, :], wdn_ref[...], preferred_element_type=F32)
          for i in range(FFN_DOWN_BLOCKS)]
    for i, f in enumerate(fs):
        res = x[i * rb:(i + 1) * rb, :] + _rms(f, wpost_ref[...])
        if natural_out:
            for r0 in range(0, rb, CHUNK):
                _store_phased(stage_refs[0], i * rb + r0, res[r0:r0 + CHUNK, :])
        else:
            _store_rows(o_ref, i * rb, res)
    if natural_out:
        _from_slabs(o_ref, stage_refs[0])


def _ffn_layer(xs, batch, layer, wpre, wup, cw, cb, wdn, wpost, natural_out):
    nl, m, _ = xs.shape
    d = nl * LANES
    seq = m // batch
    tm = min(FFN_TILE, seq)
    nt = seq // tm
    dff = wdn.shape[1]
    assert cw.shape[1] == 3
    body = functools.partial(_ffn_body, tm=tm, dff=dff, fc=FFN_COL_BLOCK, natural_out=natural_out)
    scratch = [pltpu.VMEM((PHASES * CARRY_ROWS, 2 * dff), F32), pltpu.VMEM((tm, dff), BF16)]
    if natural_out:
        scratch.append(pltpu.VMEM((nl, tm, LANES), F32))
    weights = (wpre, wup, cw, cb, wdn, wpost)
    return pl.pallas_call(
        body,
        grid=(batch, nt),
        in_specs=[_slab_spec(tm, d, nt)] + [_layer_spec(w, layer) for w in weights],
        out_specs=_stream_spec(not natural_out, tm, d, nt),
        out_shape=_stream_shape(not natural_out, m, d),
        scratch_shapes=scratch,
        compiler_params=_params(),
        name="conv_ffn",
    )(xs, *weights)


def _pool_body(x_ref, wpre_ref, wpool_ref, scale_ref, wpost_ref, o_ref, carry_ref, *, tm, dg):
    t = pl.program_id(1)

    @pl.when(t == 0)
    def _():
        carry_ref[...] = jnp.zeros_like(carry_ref)

    x = _load_rows(x_ref, 0, tm)
    h = _rms(x, wpre_ref[...])
    r = lax.broadcasted_iota(jnp.int32, (tm, 1), 0)
    pos1 = (t * tm + jnp.bitwise_and(r, -CHUNK) + _time_index(r) + 1).astype(F32)

    outs = []
    for k, w in enumerate(POOL_WINDOWS):
        cols = slice(k * dg, (k + 1) * dg)
        hg = h[:, cols]
        s, m, level = hg, 1, 0
        while m < w:
            (delayed,) = _delays(s, carry_ref.at[level], cols, (m,))
            s, m, level = s + delayed, 2 * m, level + 1
        mean = s / jnp.minimum(pos1, float(w))
        mixed = (mean - hg).astype(BF16)
        outs.append(jnp.dot(mixed, wpool_ref[k], preferred_element_type=F32))
    out = jnp.concatenate(outs, axis=1) * scale_ref[...]
    _store_rows(o_ref, 0, x + _rms(out, wpost_ref[...]))


def _pool_layer(xs, batch, layer, mixer, wpre, wpool, scale, wpost):
    nl, m, _ = xs.shape
    d = nl * LANES
    seq = m // batch
    tm = min(POOL_TILE, seq)
    nt = seq // tm
    dg = wpool.shape[2]
    assert max(POOL_WINDOWS) <= 2 * PHASES
    levels = max(POOL_WINDOWS).bit_length() - 1
    body = functools.partial(_pool_body, tm=tm, dg=dg)
    return pl.pallas_call(
        body,
        grid=(batch, nt),
        in_specs=[_slab_spec(tm, d, nt), _layer_spec(wpre, layer), _layer_spec(wpool, mixer),
                  _layer_spec(scale, mixer), _layer_spec(wpost, layer)],
        out_specs=_slab_spec(tm, d, nt),
        out_shape=_stream_shape(True, m, d),
        scratch_shapes=[pltpu.VMEM((levels, PHASES * CARRY_ROWS, d), F32)],
        compiler_params=_params(),
        name="pool_mixer",
    )(xs, wpre, wpool, scale, wpost)


def _ssd_body(x_ref, wpre_ref, win_ref, cw_ref, cb_ref, dtb_ref, alog_ref, dexp_ref, nw_ref,
              wout_ref, wpost_ref, o_ref, proj0_ref, proj1_ref, y0_ref, y1_ref, carry_ref,
              state_ref, *stage_refs, tm, d_inner, natural_in):
    q = CHUNK
    n = D_STATE
    n_chunks = tm // q
    gn = N_SSD_GROUPS * n
    gw = d_inner // N_SSD_GROUPS
    d_xbc = d_inner + 2 * gn
    off_x = d_inner
    off_b = off_x + d_inner
    off_c = off_b + gn
    off_dt = off_c + gn
    d_proj = off_dt + LANES
    d_out = wout_ref.shape[1]
    proj_refs = (proj0_ref, proj1_ref)
    y_refs = (y0_ref, y1_ref)

    @pl.when(pl.program_id(1) == 0)
    def _():
        carry_ref[...] = jnp.zeros_like(carry_ref)
        state_ref[...] = jnp.zeros_like(state_ref)

    if natural_in:
        load_chunk = functools.partial(_load_phased, stage_refs[0])
    else:
        load_chunk = functools.partial(_load_rows, x_ref, rows=q)

    a_neg = -jnp.exp(alog_ref[...]) * LOG2E
    t_row = _time_index(lax.broadcasted_iota(jnp.int32, (q, q), 0))
    t_col = _time_index(lax.broadcasted_iota(jnp.int32, (q, q), 1))
    causal = t_row >= t_col
    incl_t = jnp.where(t_row <= t_col, 1.0, 0.0).astype(BF16)
    lane = lax.broadcasted_iota(jnp.int32, (q, LANES), 1)
    left = lane < HEAD_DIM
    last = lane == q - 1


    ctxs = [{} for _ in range(n_chunks)]

    def project(ch):
        proj_ref = proj_refs[ch % 2]
        ctx = ctxs[ch]

        def norm():
            if natural_in:
                for k in range(stage_refs[0].shape[0]):
                    stage_refs[0][k, ch * q:(ch + 1) * q, :] = (
                        x_ref[ch * q:(ch + 1) * q, k * LANES:(k + 1) * LANES])
            ctx["h"] = _rms(load_chunk(ch * q), wpre_ref[...]).astype(BF16)

        def block(c0, c1):
            proj_ref[:, c0:c1] = jnp.dot(ctx["h"], win_ref[:, c0:c1], preferred_element_type=F32)

        def decays():
            v = proj_ref[:, off_dt:off_dt + LANES] + dtb_ref[...]
            dt = jnp.maximum(v, 0.0) + jnp.log1p(jnp.exp(-jnp.abs(v)))
            dt_t = dt.T
            a_dt_t = (dt * a_neg).T
            a_cum_t = sum(jnp.dot(p, incl_t, preferred_element_type=F32)
                          for p in _split_bf16(a_dt_t, 3))
            ctx["a_cum_t"] = a_cum_t
            a_last_t = jnp.sum(jnp.where(last, a_cum_t, 0.0), axis=1, keepdims=True)
            ctx["w_end_t"] = dt_t * jnp.exp2(a_last_t - a_cum_t)
            ctx["seg_sub_t"] = a_cum_t - jnp.log2(dt_t)

        return ([norm, functools.partial(block, off_dt, d_proj), decays]
                + [functools.partial(block, c0, min(c0 + PROJ_COL_BLOCK, off_dt))
                   for c0 in range(0, off_dt, PROJ_COL_BLOCK)])

    def conv(ch):
        proj_ref = proj_refs[ch % 2]

        def block(c0):
            cols = slice(c0, c0 + CONV_COL_BLOCK)
            pcols = slice(off_x + c0, off_x + c0 + CONV_COL_BLOCK)
            u = proj_ref[:, pcols]
            d1, d2, d3 = _delays(u, carry_ref, cols, (1, 2, 3))
            cw = cw_ref[:, cols]
            y = (cb_ref[:, cols] + cw[3:4, :] * u + cw[2:3, :] * d1 + cw[1:2, :] * d2
                 + cw[0:1, :] * d3)
            proj_ref[:, pcols] = y * _sigmoid(y)

        return [functools.partial(block, c0) for c0 in range(0, d_xbc, CONV_COL_BLOCK)]

    def scan(ch):
        proj_ref = proj_refs[ch % 2]
        y_ref = y_refs[ch % 2]
        ctx = ctxs[ch]
        groups = [{} for _ in range(N_SSD_GROUPS)]

        def group_start(g):
            bt = proj_ref[:, off_b + g * n:off_b + (g + 1) * n].T
            cg = proj_ref[:, off_c + g * n:off_c + (g + 1) * n].astype(BF16)
            gctx = groups[g]
            gctx["bt"] = bt
            gctx["cb"] = jnp.dot(cg, bt.astype(BF16), preferred_element_type=F32)
            gctx["s_old"] = state_ref[g]
            gctx["y_off"] = jnp.dot(cg, gctx["s_old"].astype(BF16), preferred_element_type=F32)
            gctx["pieces"] = []

        def pair(g, j):
            gctx = groups[g]
            c0 = g * gw + j * LANES
            ms, bws, e_cols = [], [], []
            for hh in range(2):
                head = c0 // HEAD_DIM + hh
                col = jnp.broadcast_to(ctx["a_cum_t"][head:head + 1, :], (q, q)).T
                seg = col - ctx["seg_sub_t"][head:head + 1, :]
                ms.append(jnp.where(causal, (gctx["cb"] * jnp.exp2(seg)).astype(BF16),
                                    jnp.zeros((), BF16)))
                bws.append((gctx["bt"] * ctx["w_end_t"][head:head + 1, :]).astype(BF16))
                e_cols.append(jnp.exp2(col))
            xp = proj_ref[:, off_x + c0:off_x + c0 + LANES]
            rhs = jnp.concatenate([jnp.where(left, xp, 0.0), jnp.where(left, 0.0, xp)],
                                  axis=0).astype(BF16)
            lhs = jnp.concatenate([jnp.concatenate(ms, axis=1),
                                   jnp.concatenate(bws, axis=1)], axis=0)
            res = jnp.dot(lhs, rhs, preferred_element_type=F32)
            e_pair = jnp.where(left, e_cols[0], e_cols[1])
            y = (res[0:q] + gctx["y_off"][:, j * LANES:(j + 1) * LANES] * e_pair
                 + dexp_ref[:, c0:c0 + LANES] * xp)
            state_ref[g, :, j * LANES:(j + 1) * LANES] = (
                gctx["s_old"][:, j * LANES:(j + 1) * LANES] * e_pair[q - 1:q, :] + res[q:2 * q])
            z = proj_ref[:, c0:c0 + LANES]
            gctx["pieces"].append(y * (z * _sigmoid(z)))

        def group_end(g):
            yg = jnp.concatenate(groups[g]["pieces"], axis=1)
            yg = yg * lax.rsqrt(jnp.mean(yg * yg, axis=-1, keepdims=True) + EPS)
            y_ref[:, g * gw:(g + 1) * gw] = (yg * nw_ref[:, g * gw:(g + 1) * gw]).astype(BF16)

        pieces = [functools.partial(group_start, 0)]
        for g in range(N_SSD_GROUPS):
            if g + 1 < N_SSD_GROUPS:
                pieces.append(functools.partial(group_start, g + 1))
            pieces += [functools.partial(pair, g, j) for j in range(gw // LANES)]
            pieces.append(functools.partial(group_end, g))
        return pieces

    def finish(ch):
        y_ref = y_refs[ch % 2]
        outs = []

        def block(c0):
            outs.append(jnp.dot(y_ref[...], wout_ref[:, c0:c0 + OUT_COL_BLOCK],
                                preferred_element_type=F32))

        def store():
            res = load_chunk(ch * q) + _rms(jnp.concatenate(outs, axis=1), wpost_ref[...])
            _store_rows(o_ref, ch * q, res)

        return [functools.partial(block, c0) for c0 in range(0, d_out, OUT_COL_BLOCK)] + [store]

    _run(project(0) + conv(0))
    for ch in range(n_chunks):
        more = ch + 1 < n_chunks
        _run(_spread(scan(ch), project(ch + 1) if more else []))
        _run(_spread(finish(ch), conv(ch + 1) if more else []))


def _ssd_layer(xin, batch, natural_in, layer, mixer, wpre, win, cw, cb, dtb, alog, dexp, nw, wout,
               wpost):
    if natural_in:
        m, d = xin.shape
        nl = d // LANES
    else:
        nl, m, _ = xin.shape
        d = nl * LANES
    seq = m // batch
    tm = min(SSD_TILE, seq)
    nt = seq // tm
    d_inner = wout.shape[1]
    d_proj = win.shape[2]
    d_xbc = cw.shape[2]
    assert cw.shape[1] == 4
    body = functools.partial(_ssd_body, tm=tm, d_inner=d_inner, natural_in=natural_in)
    scratch = [pltpu.VMEM((CHUNK, d_proj), F32),
               pltpu.VMEM((CHUNK, d_proj), F32),
               pltpu.VMEM((CHUNK, d_inner), BF16),
               pltpu.VMEM((CHUNK, d_inner), BF16),
               pltpu.VMEM((PHASES * CARRY_ROWS, d_xbc), F32),
               pltpu.VMEM((N_SSD_GROUPS, D_STATE, d_inner // N_SSD_GROUPS), F32)]
    if natural_in:
        scratch.append(pltpu.VMEM((nl, tm, LANES), F32))
    mixer_params = (win, cw, cb, dtb, alog, dexp, nw, wout)
    return pl.pallas_call(
        body,
        grid=(batch, nt),
        in_specs=([_stream_spec(not natural_in, tm, d, nt), _layer_spec(wpre, layer)]
                  + [_layer_spec(w, mixer) for w in mixer_params] + [_layer_spec(wpost, layer)]),
        out_specs=_slab_spec(tm, d, nt),
        out_shape=_stream_shape(True, m, d),
        scratch_shapes=scratch,
        compiler_params=_params(),
        name="ssd_mixer",
    )(xin, wpre, *mixer_params, wpost)


def _rows(v):
    return v.astype(F32)[:, None, :]


def kernel(x, ssd_w_in, ssd_conv_w, ssd_conv_b, ssd_dt_bias, ssd_a_log, ssd_d, ssd_norm_w, ssd_w_out, pool_w, pool_scale, ffn_w_up, ffn_conv_w, ffn_conv_b, ffn_w_down, norm_mix_pre, norm_mix_post, norm_ffn_pre, norm_ffn_post):
    batch, seq, d = x.shape
    depth = ffn_w_up.shape[0]
    n_mixers = 2
    n_heads = ssd_dt_bias.shape[1]
    assert seq % CHUNK == 0 and n_heads <= LANES and depth % n_mixers == 0
    head_pad = ((0, 0), (0, LANES - n_heads))
    ssd_params = (jnp.pad(ssd_w_in.astype(BF16), ((0, 0),) + head_pad), ssd_conv_w, _rows(ssd_conv_b),
                  _rows(jnp.pad(ssd_dt_bias, head_pad)), _rows(jnp.pad(ssd_a_log, head_pad)),
                  _rows(jnp.repeat(ssd_d, HEAD_DIM, axis=1)), _rows(ssd_norm_w), ssd_w_out.astype(BF16))
    pool_params = (pool_w.astype(BF16), _rows(pool_scale))
    ffn_params = (ffn_w_up.astype(BF16), ffn_conv_w, _rows(ffn_conv_b), ffn_w_down.astype(BF16))
    mix_pre, mix_post = _rows(norm_mix_pre), _rows(norm_mix_post)
    ffn_pre, ffn_post = _rows(norm_ffn_pre), _rows(norm_ffn_post)
    xs = x.reshape(batch * seq, d)
    for i in range(depth):
        j = i // n_mixers
        if i % n_mixers == 0:
            xs = _ssd_layer(xs, batch, i == 0, i, j, mix_pre, *ssd_params, mix_post)
        else:
            xs = _pool_layer(xs, batch, i, j, mix_pre, *pool_params, mix_post)
        xs = _ffn_layer(xs, batch, i, ffn_pre, *ffn_params, ffn_post, natural_out=i + 1 == depth)
    return xs.reshape(batch, seq, d)
```

```python
import functools

import jax
import jax.numpy as jnp
from jax import lax
from jax.experimental import pallas as pl
from jax.experimental.pallas import tpu as pltpu

F32 = jnp.float32
BF16 = jnp.bfloat16

EPS = 1e-6
LOG2E = 1.4426950408889634
HEAD_DIM = 64
D_STATE = 128
N_SSD_GROUPS = 4
CHUNK = 128
POOL_WINDOWS = (2, 4, 8, 16)
LANES = 128
CARRY_ROWS = 8
PHASES = 8
PHASE_ROWS = CHUNK // PHASES
VMEM_LIMIT_BYTES = 56 * 1024 * 1024

FFN_TILE = 1024
FFN_DOWN_BLOCKS = 4
POOL_TILE = 1024
SSD_TILE = 1024
FFN_COL_BLOCK = 256
PROJ_COL_BLOCK = 256
CONV_COL_BLOCK = 256
OUT_COL_BLOCK = 256


def _rms(x, w):
    ms = jnp.mean(x * x, axis=-1, keepdims=True)
    return x * lax.rsqrt(ms + EPS) * w


def _sigmoid(x):
    return 1.0 / (1.0 + jnp.exp2(x * -LOG2E))


def _load_rows(slab_ref, row0, rows):
    return jnp.concatenate([slab_ref[k, row0:row0 + rows, :] for k in range(slab_ref.shape[0])],
                           axis=1)


def _store_rows(slab_ref, row0, val):
    for k in range(slab_ref.shape[0]):
        slab_ref[k, row0:row0 + val.shape[0], :] = val[:, k * LANES:(k + 1) * LANES]


def _from_slabs(o_ref, slab_ref):
    for k in range(slab_ref.shape[0]):
        o_ref[:, k * LANES:(k + 1) * LANES] = slab_ref[k]


def _load_phased(slab_ref, row0, rows=CHUNK):
    rq = rows // PHASES
    return jnp.concatenate(
        [jnp.concatenate([slab_ref[k, pl.ds(row0 + p, rq, stride=PHASES), :]
                          for k in range(slab_ref.shape[0])], axis=1)
         for p in range(PHASES)], axis=0)


def _store_phased(slab_ref, row0, val):
    rq = val.shape[0] // PHASES
    for p in range(PHASES):
        for k in range(slab_ref.shape[0]):
            slab_ref[k, pl.ds(row0 + p, rq, stride=PHASES), :] = (
                val[p * rq:(p + 1) * rq, k * LANES:(k + 1) * LANES])


def _time_index(r):
    assert PHASE_ROWS & (PHASE_ROWS - 1) == 0 and CHUNK & (CHUNK - 1) == 0
    r = jnp.bitwise_and(r, CHUNK - 1)
    return (jnp.bitwise_and(r, PHASE_ROWS - 1) * PHASES
            + jnp.right_shift(r, PHASE_ROWS.bit_length() - 1))


def _delays(u, carry_ref, cols, ks, keep=None):
    n_chunks = u.shape[0] // CHUNK
    rq = PHASE_ROWS
    first = 0 if PHASES in ks else PHASES - max(ks)
    row0 = lax.broadcasted_iota(jnp.int32, (rq, 1), 0) == 0
    shifted = {}
    for c in range(n_chunks):
        for p in range(first, PHASES):
            base = c * CHUNK + p * rq
            if c == 0:
                prev_last = carry_ref[(p + 1) * CARRY_ROWS - 1:(p + 1) * CARRY_ROWS, cols]
                if keep is not None:
                    prev_last = prev_last * keep
            else:
                prev_last = u[base - CHUNK + rq - 1:base - CHUNK + rq, :]
            shifted[c, p] = jnp.where(row0, prev_last, pltpu.roll(u[base:base + rq, :], 1, axis=0))
    for p in range(first, PHASES):
        end = (n_chunks - 1) * CHUNK + (p + 1) * rq
        carry_ref[p * CARRY_ROWS:(p + 1) * CARRY_ROWS, cols] = u[end - CARRY_ROWS:end, :]
    outs = []
    for k in ks:
        pieces = []
        for c in range(n_chunks):
            pieces += [shifted[c, p] for p in range(PHASES - k, PHASES)]
            if k < PHASES:
                pieces.append(u[c * CHUNK:c * CHUNK + (PHASES - k) * rq, :])
        outs.append(jnp.concatenate(pieces, axis=0))
    return outs


def _split_bf16(x, parts):
    out = []
    rem = x
    for _ in range(parts):
        p = rem.astype(BF16)
        out.append(p)
        rem = rem - p.astype(F32)
    return out


def _spread(major, minor):
    out, done = [], 0
    for i, piece in enumerate(major):
        out.append(piece)
        want = (i + 1) * len(minor) // len(major)
        out.extend(minor[done:want])
        done = want
    return out


def _run(pieces):
    for piece in pieces:
        piece()


def _layer_spec(stacked, layer):
    tail = stacked.shape[1:]
    return pl.BlockSpec((None,) + tail, lambda b, t: (layer,) + (0,) * len(tail),
                        pipeline_mode=pl.Buffered(1))


def _row_spec(tm, d, nt):
    return pl.BlockSpec((tm, d), lambda b, t: (b * nt + t, 0))


def _slab_spec(tm, d, nt):
    return pl.BlockSpec((d // LANES, tm, LANES), lambda b, t: (0, b * nt + t, 0))


def _stream_spec(slabs, tm, d, nt):
    return _slab_spec(tm, d, nt) if slabs else _row_spec(tm, d, nt)


def _stream_shape(slabs, m, d):
    return jax.ShapeDtypeStruct((d // LANES, m, LANES) if slabs else (m, d), F32)


def _params():
    return pltpu.CompilerParams(dimension_semantics=("arbitrary", "arbitrary"),
                                vmem_limit_bytes=VMEM_LIMIT_BYTES)


def _ffn_body(x_ref, wpre_ref, wup_ref, cw_ref, cb_ref, wdn_ref, wpost_ref, o_ref,
              carry_ref, act_ref, *stage_refs, tm, dff, fc, natural_out):
    @pl.when(pl.program_id(1) == 0)
    def _():
        carry_ref[...] = jnp.zeros_like(carry_ref)

    x = _load_rows(x_ref, 0, tm)
    h = _rms(x, wpre_ref[...]).astype(BF16)

    def conv_half(col):
        cols = slice(col, col + fc)
        u = jnp.dot(h, wup_ref[:, cols], preferred_element_type=F32)
        d1, d2 = _delays(u, carry_ref, cols, (1, 2))
        cw = cw_ref[:, cols]
        return cb_ref[:, cols] + cw[2:3, :] * u + cw[1:2, :] * d1 + cw[0:1, :] * d2

    for c in range(dff // fc):
        g = conv_half(c * fc)
        v = conv_half(dff + c * fc)
        act_ref[:, c * fc:(c + 1) * fc] = (g * _sigmoid(g) * v).astype(BF16)

    rb = tm // FFN_DOWN_BLOCKS
    fs = [jnp.dot(act_ref[i * rb:(i + 1) * rb, :], wdn_ref[...], preferred_element_type=F32)
          for i in range(FFN_DOWN_BLOCKS)]
    for i, f in enumerate(fs):
        res = x[i * rb:(i + 1) * rb, :] + _rms(f, wpost_ref[...])
        if natural_out:
            for r0 in range(0, rb, CHUNK):
                _store_phased(stage_refs[0], i * rb + r0, res[r0:r0 + CHUNK, :])
        else:
            _store_rows(o_ref, i * rb, res)
    if natural_out:
        _from_slabs(o_ref, stage_refs[0])


def _ffn_layer(xs, batch, layer, wpre, wup, cw, cb, wdn, wpost, natural_out):
    nl, m, _ = xs.shape
    d = nl * LANES
    seq = m // batch
    tm = min(FFN_TILE, seq)
    nt = seq // tm
    dff = wdn.shape[1]
    assert cw.shape[1] == 3
    body = functools.partial(_ffn_body, tm=tm, dff=dff, fc=FFN_COL_BLOCK, natural_out=natural_out)
    scratch = [pltpu.VMEM((PHASES * CARRY_ROWS, 2 * dff), F32), pltpu.VMEM((tm, dff), BF16)]
    if natural_out:
        scratch.append(pltpu.VMEM((nl, tm, LANES), F32))
    weights = (wpre, wup, cw, cb, wdn, wpost)
    return pl.pallas_call(
        body,
        grid=(batch, nt),
        in_specs=[_slab_spec(tm, d, nt)] + [_layer_spec(w, layer) for w in weights],
        out_specs=_stream_spec(not natural_out, tm, d, nt),
        out_shape=_stream_shape(not natural_out, m, d),
        scratch_shapes=scratch,
        compiler_params=_params(),
        name="conv_ffn",
    )(xs, *weights)


def _pool_body(x_ref, wpre_ref, wpool_ref, scale_ref, wpost_ref, o_ref, carry_ref, *, tm, dg):
    t = pl.program_id(1)

    @pl.when(t == 0)
    def _():
        carry_ref[...] = jnp.zeros_like(carry_ref)

    x = _load_rows(x_ref, 0, tm)
    h = _rms(x, wpre_ref[...])
    r = lax.broadcasted_iota(jnp.int32, (tm, 1), 0)
    pos1 = (t * tm + jnp.bitwise_and(r, -CHUNK) + _time_index(r) + 1).astype(F32)

    outs = []
    for k, w in enumerate(POOL_WINDOWS):
        cols = slice(k * dg, (k + 1) * dg)
        hg = h[:, cols]
        s, m, level = hg, 1, 0
        while m < w:
            (delayed,) = _delays(s, carry_ref.at[level], cols, (m,))
            s, m, level = s + delayed, 2 * m, level + 1
        mean = s / jnp.minimum(pos1, float(w))
        mixed = (mean - hg).astype(BF16)
        outs.append(jnp.dot(mixed, wpool_ref[k], preferred_element_type=F32))
    out = jnp.concatenate(outs, axis=1) * scale_ref[...]
    _store_rows(o_ref, 0, x + _rms(out, wpost_ref[...]))


def _pool_layer(xs, batch, layer, mixer, wpre, wpool, scale, wpost):
    nl, m, _ = xs.shape
    d = nl * LANES
    seq = m // batch
    tm = min(POOL_TILE, seq)
    nt = seq // tm
    dg = wpool.shape[2]
    assert max(POOL_WINDOWS) <= 2 * PHASES
    levels = max(POOL_WINDOWS).bit_length() - 1
    body = functools.partial(_pool_body, tm=tm, dg=dg)
    return pl.pallas_call(
        body,
        grid=(batch, nt),
        in_specs=[_slab_spec(tm, d, nt), _layer_spec(wpre, layer), _layer_spec(wpool, mixer),
                  _layer_spec(scale, mixer), _layer_spec(wpost, layer)],
        out_specs=_slab_spec(tm, d, nt),
        out_shape=_stream_shape(True, m, d),
        scratch_shapes=[pltpu.VMEM((levels, PHASES * CARRY_ROWS, d), F32)],
        compiler_params=_params(),
        name="pool_mixer",
    )(xs, wpre, wpool, scale, wpost)


def _ssd_body(x_ref, xnext_ref, wpre_ref, win_ref, cw_ref, cb_ref, dtb_ref, alog_ref, dexp_ref,
              nw_ref, wout_ref, wpost_ref, o_ref, proj0_ref, proj1_ref, y0_ref, y1_ref, carry_ref,
              state_ref, dec_ref, *stage_refs, tm, nt, d_inner, natural_in):
    q = CHUNK
    n = D_STATE
    n_chunks = tm // q
    gn = N_SSD_GROUPS * n
    gw = d_inner // N_SSD_GROUPS
    d_xbc = d_inner + 2 * gn
    off_x = d_inner
    off_b = off_x + d_inner
    off_c = off_b + gn
    off_dt = off_c + gn
    d_proj = off_dt + LANES
    d_out = wout_ref.shape[1]
    proj_refs = (proj0_ref, proj1_ref)
    y_refs = (y0_ref, y1_ref)

    assert n_chunks % 2 == 0
    t = pl.program_id(1)
    first_step = (pl.program_id(0) == 0) & (t == 0)
    keep_tail = jnp.where(t == nt - 1, 0.0, 1.0)

    @pl.when(t == 0)
    def _():
        state_ref[...] = jnp.zeros_like(state_ref)

    def stage_chunk(ch):
        rows = xnext_ref if ch == n_chunks else x_ref.at[ch * q:(ch + 1) * q, :]
        for k in range(stage_refs[0].shape[0]):
            stage_refs[0][k, ch * q:(ch + 1) * q, :] = rows[:, k * LANES:(k + 1) * LANES]

    def load_chunk(ch):
        if natural_in:
            return _load_phased(stage_refs[0], ch * q)
        if ch == n_chunks:
            return _load_rows(xnext_ref, 0, q)
        return _load_rows(x_ref, ch * q, q)

    a_neg = -jnp.exp(alog_ref[...]) * LOG2E
    t_row = _time_index(lax.broadcasted_iota(jnp.int32, (q, q), 0))
    t_col = _time_index(lax.broadcasted_iota(jnp.int32, (q, q), 1))
    causal = t_row >= t_col
    incl_t = jnp.where(t_row <= t_col, 1.0, 0.0).astype(BF16)
    lane = lax.broadcasted_iota(jnp.int32, (q, LANES), 1)
    left = lane < HEAD_DIM
    last = lane == q - 1


    def project(ch):
        proj_ref = proj_refs[ch % 2]
        dec = dec_ref.at[ch % 2]
        ctx = {}

        def norm():
            if natural_in:
                stage_chunk(ch)
            ctx["h"] = _rms(load_chunk(ch), wpre_ref[...]).astype(BF16)

        def block(c0, c1):
            proj_ref[:, c0:c1] = jnp.dot(ctx["h"], win_ref[:, c0:c1], preferred_element_type=F32)

        def decays():
            v = proj_ref[:, off_dt:off_dt + LANES] + dtb_ref[...]
            dt = jnp.maximum(v, 0.0) + jnp.log1p(jnp.exp(-jnp.abs(v)))
            dt_t = dt.T
            a_dt_t = (dt * a_neg).T
            a_cum_t = sum(jnp.dot(p, incl_t, preferred_element_type=F32)
                          for p in _split_bf16(a_dt_t, 3))
            dec[0] = a_cum_t
            a_last_t = jnp.sum(jnp.where(last, a_cum_t, 0.0), axis=1, keepdims=True)
            dec[1] = dt_t * jnp.exp2(a_last_t - a_cum_t)
            dec[2] = a_cum_t - jnp.log2(dt_t)

        return ([norm, functools.partial(block, off_dt, d_proj), decays]
                + [functools.partial(block, c0, min(c0 + PROJ_COL_BLOCK, off_dt))
                   for c0 in range(0, off_dt, PROJ_COL_BLOCK)])

    def conv(ch):
        proj_ref = proj_refs[ch % 2]

        def block(c0):
            cols = slice(c0, c0 + CONV_COL_BLOCK)
            pcols = slice(off_x + c0, off_x + c0 + CONV_COL_BLOCK)
            u = proj_ref[:, pcols]
            d1, d2, d3 = _delays(u, carry_ref, cols, (1, 2, 3),
                                 keep=keep_tail if ch == n_chunks else None)
            cw = cw_ref[:, cols]
            y = (cb_ref[:, cols] + cw[3:4, :] * u + cw[2:3, :] * d1 + cw[1:2, :] * d2
                 + cw[0:1, :] * d3)
            proj_ref[:, pcols] = y * _sigmoid(y)

        return [functools.partial(block, c0) for c0 in range(0, d_xbc, CONV_COL_BLOCK)]

    def scan(ch):
        proj_ref = proj_refs[ch % 2]
        y_ref = y_refs[ch % 2]
        dec = dec_ref.at[ch % 2]
        groups = [{} for _ in range(N_SSD_GROUPS)]

        def group_start(g):
            bt = proj_ref[:, off_b + g * n:off_b + (g + 1) * n].T
            cg = proj_ref[:, off_c + g * n:off_c + (g + 1) * n].astype(BF16)
            gctx = groups[g]
            gctx["bt"] = bt
            gctx["cb"] = jnp.dot(cg, bt.astype(BF16), preferred_element_type=F32)
            gctx["s_old"] = state_ref[g]
            gctx["y_off"] = jnp.dot(cg, gctx["s_old"].astype(BF16), preferred_element_type=F32)
            gctx["pieces"] = []

        def pair(g, j):
            gctx = groups[g]
            c0 = g * gw + j * LANES
            ms, bws, e_cols = [], [], []
            for hh in range(2):
                head = c0 // HEAD_DIM + hh
                col = jnp.broadcast_to(dec[0, head:head + 1, :], (q, q)).T
                seg = col - dec[2, head:head + 1, :]
                ms.append(jnp.where(causal, (gctx["cb"] * jnp.exp2(seg)).astype(BF16),
                                    jnp.zeros((), BF16)))
                bws.append((gctx["bt"] * dec[1, head:head + 1, :]).astype(BF16))
                e_cols.append(jnp.exp2(col))
            xp = proj_ref[:, off_x + c0:off_x + c0 + LANES]
            rhs = jnp.concatenate([jnp.where(left, xp, 0.0), jnp.where(left, 0.0, xp)],
                                  axis=0).astype(BF16)
            lhs = jnp.concatenate([jnp.concatenate(ms, axis=1),
                                   jnp.concatenate(bws, axis=1)], axis=0)
            res = jnp.dot(lhs, rhs, preferred_element_type=F32)
            e_pair = jnp.where(left, e_cols[0], e_cols[1])
            y = (res[0:q] + gctx["y_off"][:, j * LANES:(j + 1) * LANES] * e_pair
                 + dexp_ref[:, c0:c0 + LANES] * xp)
            state_ref[g, :, j * LANES:(j + 1) * LANES] = (
                gctx["s_old"][:, j * LANES:(j + 1) * LANES] * e_pair[q - 1:q, :] + res[q:2 * q])
            z = proj_ref[:, c0:c0 + LANES]
            gctx["pieces"].append(y * (z * _sigmoid(z)))

        def group_end(g):
            yg = jnp.concatenate(groups[g]["pieces"], axis=1)
            yg = yg * lax.rsqrt(jnp.mean(yg * yg, axis=-1, keepdims=True) + EPS)
            y_ref[:, g * gw:(g + 1) * gw] = (yg * nw_ref[:, g * gw:(g + 1) * gw]).astype(BF16)

        pieces = [functools.partial(group_start, 0)]
        for g in range(N_SSD_GROUPS):
            if g + 1 < N_SSD_GROUPS:
                pieces.append(functools.partial(group_start, g + 1))
            pieces += [functools.partial(pair, g, j) for j in range(gw // LANES)]
            pieces.append(functools.partial(group_end, g))
        return pieces

    def finish(ch):
        y_ref = y_refs[ch % 2]
        outs = []

        def block(c0):
            outs.append(jnp.dot(y_ref[...], wout_ref[:, c0:c0 + OUT_COL_BLOCK],
                                preferred_element_type=F32))

        def store():
            if natural_in and ch == 0:
                stage_chunk(0)
            res = load_chunk(ch) + _rms(jnp.concatenate(outs, axis=1), wpost_ref[...])
            _store_rows(o_ref, ch * q, res)

        return [functools.partial(block, c0) for c0 in range(0, d_out, OUT_COL_BLOCK)] + [store]

    @pl.when(first_step)
    def _():
        carry_ref[...] = jnp.zeros_like(carry_ref)
        _run(project(0) + conv(0))

    for ch in range(n_chunks):
        _run(_spread(scan(ch), project(ch + 1)))
        _run(_spread(finish(ch), conv(ch + 1)))


def _ssd_layer(xin, batch, natural_in, layer, mixer, wpre, win, cw, cb, dtb, alog, dexp, nw, wout,
               wpost):
    if natural_in:
        m, d = xin.shape
        nl = d // LANES
    else:
        nl, m, _ = xin.shape
        d = nl * LANES
    seq = m // batch
    tm = min(SSD_TILE, seq)
    nt = seq // tm
    d_inner = wout.shape[1]
    d_proj = win.shape[2]
    d_xbc = cw.shape[2]
    assert cw.shape[1] == 4
    body = functools.partial(_ssd_body, tm=tm, nt=nt, d_inner=d_inner, natural_in=natural_in)
    scratch = [pltpu.VMEM((CHUNK, d_proj), F32),
               pltpu.VMEM((CHUNK, d_proj), F32),
               pltpu.VMEM((CHUNK, d_inner), BF16),
               pltpu.VMEM((CHUNK, d_inner), BF16),
               pltpu.VMEM((PHASES * CARRY_ROWS, d_xbc), F32),
               pltpu.VMEM((N_SSD_GROUPS, D_STATE, d_inner // N_SSD_GROUPS), F32),
               pltpu.VMEM((2, 3, CHUNK, CHUNK), F32)]
    if natural_in:
        scratch.append(pltpu.VMEM((nl, tm + CHUNK, LANES), F32))

    chunks_per_tile, last_chunk = tm // CHUNK, m // CHUNK - 1

    def next_chunk(b, t):
        return jnp.minimum((b * nt + t + 1) * chunks_per_tile, last_chunk)

    if natural_in:
        next_spec = pl.BlockSpec((CHUNK, d), lambda b, t: (next_chunk(b, t), 0))
    else:
        next_spec = pl.BlockSpec((nl, CHUNK, LANES), lambda b, t: (0, next_chunk(b, t), 0))
    mixer_params = (win, cw, cb, dtb, alog, dexp, nw, wout)
    return pl.pallas_call(
        body,
        grid=(batch, nt),
        in_specs=([_stream_spec(not natural_in, tm, d, nt), next_spec, _layer_spec(wpre, layer)]
                  + [_layer_spec(w, mixer) for w in mixer_params] + [_layer_spec(wpost, layer)]),
        out_specs=_slab_spec(tm, d, nt),
        out_shape=_stream_shape(True, m, d),
        scratch_shapes=scratch,
        compiler_params=_params(),
        name="ssd_mixer",
    )(xin, xin, wpre, *mixer_params, wpost)


def _rows(v):
    return v.astype(F32)[:, None, :]


def kernel(x, ssd_w_in, ssd_conv_w, ssd_conv_b, ssd_dt_bias, ssd_a_log, ssd_d, ssd_norm_w, ssd_w_out, pool_w, pool_scale, ffn_w_up, ffn_conv_w, ffn_conv_b, ffn_w_down, norm_mix_pre, norm_mix_post, norm_ffn_pre, norm_ffn_post):
    batch, seq, d = x.shape
    depth = ffn_w_up.shape[0]
    n_mixers = 2
    n_heads = ssd_dt_bias.shape[1]
    assert seq % CHUNK == 0 and n_heads <= LANES and depth % n_mixers == 0
    head_pad = ((0, 0), (0, LANES - n_heads))
    ssd_params = (jnp.pad(ssd_w_in.astype(BF16), ((0, 0),) + head_pad), ssd_conv_w, _rows(ssd_conv_b),
                  _rows(jnp.pad(ssd_dt_bias, head_pad)), _rows(jnp.pad(ssd_a_log, head_pad)),
                  _rows(jnp.repeat(ssd_d, HEAD_DIM, axis=1)), _rows(ssd_norm_w), ssd_w_out.astype(BF16))
    pool_params = (pool_w.astype(BF16), _rows(pool_scale))
    ffn_params = (ffn_w_up.astype(BF16), ffn_conv_w, _rows(ffn_conv_b), ffn_w_down.astype(BF16))
    mix_pre, mix_post = _rows(norm_mix_pre), _rows(norm_mix_post)
    ffn_pre, ffn_post = _rows(norm_ffn_pre), _rows(norm_ffn_post)
    xs = x.reshape(batch * seq, d)
    for i in range(depth):
        j = i // n_mixers
        if i % n_mixers == 0:
            xs = _ssd_layer(xs, batch, i == 0, i, j, mix_pre, *ssd_params, mix_post)
        else:
            xs = _pool_layer(xs, batch, i, j, mix_pre, *pool_params, mix_post)
        xs = _ffn_layer(xs, batch, i, ffn_pre, *ffn_params, ffn_post, natural_out=i + 1 == depth)
    return xs.reshape(batch, seq, d)
```

```python
import functools

import jax
import jax.numpy as jnp
from jax import lax
from jax.experimental import pallas as pl
from jax.experimental.pallas import tpu as pltpu

F32 = jnp.float32
BF16 = jnp.bfloat16

EPS = 1e-6
LOG2E = 1.4426950408889634
HEAD_DIM = 64
D_STATE = 128
N_SSD_GROUPS = 4
CHUNK = 128
POOL_WINDOWS = (2, 4, 8, 16)
LANES = 128
CARRY_ROWS = 8
PHASES = 8
PHASE_ROWS = CHUNK // PHASES
VMEM_LIMIT_BYTES = 56 * 1024 * 1024

FFN_TILE = 1024
FFN_DOWN_BLOCKS = 4
POOL_TILE = 1024
SSD_TILE = 1024
FFN_COL_BLOCK = 256
PROJ_COL_BLOCK = 256
CONV_COL_BLOCK = 256
OUT_COL_BLOCK = 256


def _rms(x, w):
    ms = jnp.mean(x * x, axis=-1, keepdims=True)
    return x * lax.rsqrt(ms + EPS) * w


def _sigmoid(x):
    return 1.0 / (1.0 + jnp.exp2(x * -LOG2E))


def _load_rows(slab_ref, row0, rows):
    return jnp.concatenate([slab_ref[k, row0:row0 + rows, :] for k in range(slab_ref.shape[0])],
                           axis=1)


def _store_rows(slab_ref, row0, val):
    for k in range(slab_ref.shape[0]):
        slab_ref[k, row0:row0 + val.shape[0], :] = val[:, k * LANES:(k + 1) * LANES]


def _from_slabs(o_ref, slab_ref):
    for k in range(slab_ref.shape[0]):
        o_ref[:, k * LANES:(k + 1) * LANES] = slab_ref[k]


def _load_phased(slab_ref, row0, rows=CHUNK):
    rq = rows // PHASES
    return jnp.concatenate(
        [jnp.concatenate([slab_ref[k, pl.ds(row0 + p, rq, stride=PHASES), :]
                          for k in range(slab_ref.shape[0])], axis=1)
         for p in range(PHASES)], axis=0)


def _store_phased(slab_ref, row0, val):
    rq = val.shape[0] // PHASES
    for p in range(PHASES):
        for k in range(slab_ref.shape[0]):
            slab_ref[k, pl.ds(row0 + p, rq, stride=PHASES), :] = (
                val[p * rq:(p + 1) * rq, k * LANES:(k + 1) * LANES])


def _time_index(r):
    assert PHASE_ROWS & (PHASE_ROWS - 1) == 0 and CHUNK & (CHUNK - 1) == 0
    r = jnp.bitwise_and(r, CHUNK - 1)
    return (jnp.bitwise_and(r, PHASE_ROWS - 1) * PHASES
            + jnp.right_shift(r, PHASE_ROWS.bit_length() - 1))


def _delays(u, carry_ref, cols, ks):
    n_chunks = u.shape[0] // CHUNK
    rq = PHASE_ROWS
    first = 0 if PHASES in ks else PHASES - max(ks)
    row0 = lax.broadcasted_iota(jnp.int32, (rq, 1), 0) == 0
    shifted = {}
    for c in range(n_chunks):
        for p in range(first, PHASES):
            base = c * CHUNK + p * rq
            if c == 0:
                prev_last = carry_ref[(p + 1) * CARRY_ROWS - 1:(p + 1) * CARRY_ROWS, cols]
            else:
                prev_last = u[base - CHUNK + rq - 1:base - CHUNK + rq, :]
            shifted[c, p] = jnp.where(row0, prev_last, pltpu.roll(u[base:base + rq, :], 1, axis=0))
    for p in range(first, PHASES):
        end = (n_chunks - 1) * CHUNK + (p + 1) * rq
        carry_ref[p * CARRY_ROWS:(p + 1) * CARRY_ROWS, cols] = u[end - CARRY_ROWS:end, :]
    outs = []
    for k in ks:
        pieces = []
        for c in range(n_chunks):
            pieces += [shifted[c, p] for p in range(PHASES - k, PHASES)]
            if k < PHASES:
                pieces.append(u[c * CHUNK:c * CHUNK + (PHASES - k) * rq, :])
        outs.append(jnp.concatenate(pieces, axis=0))
    return outs


def _split_bf16(x, parts):
    out = []
    rem = x
    for _ in range(parts):
        p = rem.astype(BF16)
        out.append(p)
        rem = rem - p.astype(F32)
    return out


def _spread(major, minor):
    out, done = [], 0
    for i, piece in enumerate(major):
        out.append(piece)
        want = (i + 1) * len(minor) // len(major)
        out.extend(minor[done:want])
        done = want
    return out


def _run(pieces):
    for piece in pieces:
        piece()


def _layer_spec(stacked, layer):
    tail = stacked.shape[1:]
    return pl.BlockSpec((None,) + tail, lambda b, t: (layer,) + (0,) * len(tail),
                        pipeline_mode=pl.Buffered(1))


def _row_spec(tm, d, nt):
    return pl.BlockSpec((tm, d), lambda b, t: (b * nt + t, 0))


def _slab_spec(tm, d, nt):
    return pl.BlockSpec((d // LANES, tm, LANES), lambda b, t: (0, b * nt + t, 0))


def _stream_spec(slabs, tm, d, nt):
    return _slab_spec(tm, d, nt) if slabs else _row_spec(tm, d, nt)


def _stream_shape(slabs, m, d):
    return jax.ShapeDtypeStruct((d // LANES, m, LANES) if slabs else (m, d), F32)


def _params():
    return pltpu.CompilerParams(dimension_semantics=("arbitrary", "arbitrary"),
                                vmem_limit_bytes=VMEM_LIMIT_BYTES)


def _ffn_body(x_ref, wpre_ref, wup_ref, cw_ref, cb_ref, wdn_ref, wpost_ref, o_ref,
              carry_ref, act_ref, *stage_refs, tm, dff, fc, natural_out):
    @pl.when(pl.program_id(1) == 0)
    def _():
        carry_ref[...] = jnp.zeros_like(carry_ref)

    x = _load_rows(x_ref, 0, tm)
    h = _rms(x, wpre_ref[...]).astype(BF16)

    def conv_half(col):
        cols = slice(col, col + fc)
        u = jnp.dot(h, wup_ref[:, cols], preferred_element_type=F32)
        d1, d2 = _delays(u, carry_ref, cols, (1, 2))
        cw = cw_ref[:, cols]
        return cb_ref[:, cols] + cw[2:3, :] * u + cw[1:2, :] * d1 + cw[0:1, :] * d2

    for c in range(dff // fc):
        g = conv_half(c * fc)
        v = conv_half(dff + c * fc)
        act_ref[:, c * fc:(c + 1) * fc] = (g * _sigmoid(g) * v).astype(BF16)

    rb = tm // FFN_DOWN_BLOCKS
    fs = [jnp.dot(act_ref[i * rb:(i + 1) * rb, :], wdn_ref[...], preferred_element_type=F32)
          for i in range(FFN_DOWN_BLOCKS)]
    for i, f in enumerate(fs):
        res = _load_rows(x_ref, i * rb, rb) + _rms(f, wpost_ref[...])
        if natural_out:
            for r0 in range(0, rb, CHUNK):
                _store_phased(stage_refs[0], i * rb + r0, res[r0:r0 + CHUNK, :])
        else:
            _store_rows(o_ref, i * rb, res)
    if natural_out:
        _from_slabs(o_ref, stage_refs[0])


def _ffn_layer(xs, batch, layer, wpre, wup, cw, cb, wdn, wpost, natural_out):
    nl, m, _ = xs.shape
    d = nl * LANES
    seq = m // batch
    tm = min(FFN_TILE, seq)
    nt = seq // tm
    dff = wdn.shape[1]
    assert cw.shape[1] == 3
    body = functools.partial(_ffn_body, tm=tm, dff=dff, fc=FFN_COL_BLOCK, natural_out=natural_out)
    scratch = [pltpu.VMEM((PHASES * CARRY_ROWS, 2 * dff), F32), pltpu.VMEM((tm, dff), BF16)]
    if natural_out:
        scratch.append(pltpu.VMEM((nl, tm, LANES), F32))
    weights = (wpre, wup, cw, cb, wdn, wpost)
    return pl.pallas_call(
        body,
        grid=(batch, nt),
        in_specs=[_slab_spec(tm, d, nt)] + [_layer_spec(w, layer) for w in weights],
        out_specs=_stream_spec(not natural_out, tm, d, nt),
        out_shape=_stream_shape(not natural_out, m, d),
        scratch_shapes=scratch,
        compiler_params=_params(),
        name="conv_ffn",
    )(xs, *weights)


def _pool_body(x_ref, wpre_ref, wpool_ref, scale_ref, wpost_ref, o_ref, carry_ref, *, tm, dg):
    t = pl.program_id(1)

    @pl.when(t == 0)
    def _():
        carry_ref[...] = jnp.zeros_like(carry_ref)

    x = _load_rows(x_ref, 0, tm)
    h = _rms(x, wpre_ref[...])
    r = lax.broadcasted_iota(jnp.int32, (tm, 1), 0)
    pos1 = (t * tm + jnp.bitwise_and(r, -CHUNK) + _time_index(r) + 1).astype(F32)

    outs = []
    for k, w in enumerate(POOL_WINDOWS):
        cols = slice(k * dg, (k + 1) * dg)
        hg = h[:, cols]
        s, m, level = hg, 1, 0
        while m < w:
            (delayed,) = _delays(s, carry_ref.at[level], cols, (m,))
            s, m, level = s + delayed, 2 * m, level + 1
        mean = s / jnp.minimum(pos1, float(w))
        mixed = (mean - hg).astype(BF16)
        outs.append(jnp.dot(mixed, wpool_ref[k], preferred_element_type=F32))
    out = jnp.concatenate(outs, axis=1) * scale_ref[...]
    _store_rows(o_ref, 0, x + _rms(out, wpost_ref[...]))


def _pool_layer(xs, batch, layer, mixer, wpre, wpool, scale, wpost):
    nl, m, _ = xs.shape
    d = nl * LANES
    seq = m // batch
    tm = min(POOL_TILE, seq)
    nt = seq // tm
    dg = wpool.shape[2]
    assert max(POOL_WINDOWS) <= 2 * PHASES
    levels = max(POOL_WINDOWS).bit_length() - 1
    body = functools.partial(_pool_body, tm=tm, dg=dg)
    return pl.pallas_call(
        body,
        grid=(batch, nt),
        in_specs=[_slab_spec(tm, d, nt), _layer_spec(wpre, layer), _layer_spec(wpool, mixer),
                  _layer_spec(scale, mixer), _layer_spec(wpost, layer)],
        out_specs=_slab_spec(tm, d, nt),
        out_shape=_stream_shape(True, m, d),
        scratch_shapes=[pltpu.VMEM((levels, PHASES * CARRY_ROWS, d), F32)],
        compiler_params=_params(),
        name="pool_mixer",
    )(xs, wpre, wpool, scale, wpost)


def _ssd_body(x_ref, wpre_ref, win_ref, cw_ref, cb_ref, dtb_ref, alog_ref, dexp_ref, nw_ref,
              wout_ref, wpost_ref, o_ref, proj0_ref, proj1_ref, y0_ref, y1_ref, carry_ref,
              state_ref, *stage_refs, tm, d_inner, natural_in):
    q = CHUNK
    n = D_STATE
    n_chunks = tm // q
    gn = N_SSD_GROUPS * n
    gw = d_inner // N_SSD_GROUPS
    d_xbc = d_inner + 2 * gn
    off_x = d_inner
    off_b = off_x + d_inner
    off_c = off_b + gn
    off_dt = off_c + gn
    d_proj = off_dt + LANES
    d_out = wout_ref.shape[1]
    proj_refs = (proj0_ref, proj1_ref)
    y_refs = (y0_ref, y1_ref)

    @pl.when(pl.program_id(1) == 0)
    def _():
        carry_ref[...] = jnp.zeros_like(carry_ref)
        state_ref[...] = jnp.zeros_like(state_ref)

    if natural_in:
        load_chunk = functools.partial(_load_phased, stage_refs[0])
    else:
        load_chunk = functools.partial(_load_rows, x_ref, rows=q)

    a_neg = -jnp.exp(alog_ref[...]) * LOG2E
    t_row = _time_index(lax.broadcasted_iota(jnp.int32, (q, q), 0))
    t_col = _time_index(lax.broadcasted_iota(jnp.int32, (q, q), 1))
    causal = t_row >= t_col
    incl_t = jnp.where(t_row <= t_col, 1.0, 0.0).astype(BF16)
    lane = lax.broadcasted_iota(jnp.int32, (q, LANES), 1)
    left = lane < HEAD_DIM
    last = lane == q - 1


    ctxs = [{} for _ in range(n_chunks)]

    def project(ch):
        proj_ref = proj_refs[ch % 2]
        ctx = ctxs[ch]

        def norm():
            if natural_in:
                for k in range(stage_refs[0].shape[0]):
                    stage_refs[0][k, ch * q:(ch + 1) * q, :] = (
                        x_ref[ch * q:(ch + 1) * q, k * LANES:(k + 1) * LANES])
            xc = load_chunk(ch * q)
            if natural_in:
                _store_rows(o_ref, ch * q, xc)
            ctx["h"] = _rms(xc, wpre_ref[...]).astype(BF16)

        def block(c0, c1):
            proj_ref[:, c0:c1] = jnp.dot(ctx["h"], win_ref[:, c0:c1], preferred_element_type=F32)

        def decays():
            v = proj_ref[:, off_dt:off_dt + LANES] + dtb_ref[...]
            dt = jnp.maximum(v, 0.0) + jnp.log1p(jnp.exp(-jnp.abs(v)))
            dt_t = dt.T
            a_dt_t = (dt * a_neg).T
            a_cum_t = sum(jnp.dot(p, incl_t, preferred_element_type=F32)
                          for p in _split_bf16(a_dt_t, 3))
            ctx["a_cum_t"] = a_cum_t
            a_last_t = jnp.sum(jnp.where(last, a_cum_t, 0.0), axis=1, keepdims=True)
            ctx["w_end_t"] = dt_t * jnp.exp2(a_last_t - a_cum_t)
            ctx["seg_sub_t"] = a_cum_t - jnp.log2(dt_t)

        return ([norm, functools.partial(block, off_dt, d_proj), decays]
                + [functools.partial(block, c0, min(c0 + PROJ_COL_BLOCK, off_dt))
                   for c0 in range(0, off_dt, PROJ_COL_BLOCK)])

    def conv(ch):
        proj_ref = proj_refs[ch % 2]

        def block(c0):
            cols = slice(c0, c0 + CONV_COL_BLOCK)
            pcols = slice(off_x + c0, off_x + c0 + CONV_COL_BLOCK)
            u = proj_ref[:, pcols]
            d1, d2, d3 = _delays(u, carry_ref, cols, (1, 2, 3))
            cw = cw_ref[:, cols]
            y = (cb_ref[:, cols] + cw[3:4, :] * u + cw[2:3, :] * d1 + cw[1:2, :] * d2
                 + cw[0:1, :] * d3)
            proj_ref[:, pcols] = y * _sigmoid(y)

        return [functools.partial(block, c0) for c0 in range(0, d_xbc, CONV_COL_BLOCK)]

    def scan(ch):
        proj_ref = proj_refs[ch % 2]
        y_ref = y_refs[ch % 2]
        ctx = ctxs[ch]
        groups = [{} for _ in range(N_SSD_GROUPS)]

        def group_start(g):
            bt = proj_ref[:, off_b + g * n:off_b + (g + 1) * n].T
            cg = proj_ref[:, off_c + g * n:off_c + (g + 1) * n].astype(BF16)
            gctx = groups[g]
            gctx["bt"] = bt
            gctx["cb"] = jnp.dot(cg, bt.astype(BF16), preferred_element_type=F32)
            gctx["s_old"] = state_ref[g]
            gctx["y_off"] = jnp.dot(cg, gctx["s_old"].astype(BF16), preferred_element_type=F32)
            gctx["pieces"] = []

        def pair(g, j):
            gctx = groups[g]
            c0 = g * gw + j * LANES
            ms, bws, e_cols = [], [], []
            for hh in range(2):
                head = c0 // HEAD_DIM + hh
                col = jnp.broadcast_to(ctx["a_cum_t"][head:head + 1, :], (q, q)).T
                seg = col - ctx["seg_sub_t"][head:head + 1, :]
                ms.append(jnp.where(causal, (gctx["cb"] * jnp.exp2(seg)).astype(BF16),
                                    jnp.zeros((), BF16)))
                bws.append((gctx["bt"] * ctx["w_end_t"][head:head + 1, :]).astype(BF16))
                e_cols.append(jnp.exp2(col))
            xp = proj_ref[:, off_x + c0:off_x + c0 + LANES]
            rhs = jnp.concatenate([jnp.where(left, xp, 0.0), jnp.where(left, 0.0, xp)],
                                  axis=0).astype(BF16)
            lhs = jnp.concatenate([jnp.concatenate(ms, axis=1),
                                   jnp.concatenate(bws, axis=1)], axis=0)
            res = jnp.dot(lhs, rhs, preferred_element_type=F32)
            e_pair = jnp.where(left, e_cols[0], e_cols[1])
            y = (res[0:q] + gctx["y_off"][:, j * LANES:(j + 1) * LANES] * e_pair
                 + dexp_ref[:, c0:c0 + LANES] * xp)
            state_ref[g, :, j * LANES:(j + 1) * LANES] = (
                gctx["s_old"][:, j * LANES:(j + 1) * LANES] * e_pair[q - 1:q, :] + res[q:2 * q])
            z = proj_ref[:, c0:c0 + LANES]
            gctx["pieces"].append(y * (z * _sigmoid(z)))

        def group_end(g):
            yg = jnp.concatenate(groups[g]["pieces"], axis=1)
            yg = yg * lax.rsqrt(jnp.mean(yg * yg, axis=-1, keepdims=True) + EPS)
            y_ref[:, g * gw:(g + 1) * gw] = (yg * nw_ref[:, g * gw:(g + 1) * gw]).astype(BF16)

        pieces = [functools.partial(group_start, 0)]
        for g in range(N_SSD_GROUPS):
            if g + 1 < N_SSD_GROUPS:
                pieces.append(functools.partial(group_start, g + 1))
            pieces += [functools.partial(pair, g, j) for j in range(gw // LANES)]
            pieces.append(functools.partial(group_end, g))
        return pieces

    def finish(ch):
        y_ref = y_refs[ch % 2]
        outs = []

        def block(c0):
            outs.append(jnp.dot(y_ref[...], wout_ref[:, c0:c0 + OUT_COL_BLOCK],
                                preferred_element_type=F32))

        def store():
            xc = _load_rows(o_ref, ch * q, q) if natural_in else load_chunk(ch * q)
            res = xc + _rms(jnp.concatenate(outs, axis=1), wpost_ref[...])
            _store_rows(o_ref, ch * q, res)

        return [functools.partial(block, c0) for c0 in range(0, d_out, OUT_COL_BLOCK)] + [store]

    _run(project(0) + conv(0))
    for ch in range(n_chunks):
        more = ch + 1 < n_chunks
        _run(_spread(scan(ch), project(ch + 1) if more else []))
        _run(_spread(finish(ch), conv(ch + 1) if more else []))


def _ssd_layer(xin, batch, natural_in, layer, mixer, wpre, win, cw, cb, dtb, alog, dexp, nw, wout,
               wpost):
    if natural_in:
        m, d = xin.shape
        nl = d // LANES
    else:
        nl, m, _ = xin.shape
        d = nl * LANES
    seq = m // batch
    tm = min(SSD_TILE, seq)
    nt = seq // tm
    d_inner = wout.shape[1]
    d_proj = win.shape[2]
    d_xbc = cw.shape[2]
    assert cw.shape[1] == 4
    body = functools.partial(_ssd_body, tm=tm, d_inner=d_inner, natural_in=natural_in)
    scratch = [pltpu.VMEM((CHUNK, d_proj), F32),
               pltpu.VMEM((CHUNK, d_proj), F32),
               pltpu.VMEM((CHUNK, d_inner), BF16),
               pltpu.VMEM((CHUNK, d_inner), BF16),
               pltpu.VMEM((PHASES * CARRY_ROWS, d_xbc), F32),
               pltpu.VMEM((N_SSD_GROUPS, D_STATE, d_inner // N_SSD_GROUPS), F32)]
    if natural_in:
        scratch.append(pltpu.VMEM((nl, tm, LANES), F32))
    mixer_params = (win, cw, cb, dtb, alog, dexp, nw, wout)
    return pl.pallas_call(
        body,
        grid=(batch, nt),
        in_specs=([_stream_spec(not natural_in, tm, d, nt), _layer_spec(wpre, layer)]
                  + [_layer_spec(w, mixer) for w in mixer_params] + [_layer_spec(wpost, layer)]),
        out_specs=_slab_spec(tm, d, nt),
        out_shape=_stream_shape(True, m, d),
        scratch_shapes=scratch,
        compiler_params=_params(),
        name="ssd_mixer",
    )(xin, wpre, *mixer_params, wpost)


def _rows(v):
    return v.astype(F32)[:, None, :]


def kernel(x, ssd_w_in, ssd_conv_w, ssd_conv_b, ssd_dt_bias, ssd_a_log, ssd_d, ssd_norm_w, ssd_w_out, pool_w, pool_scale, ffn_w_up, ffn_conv_w, ffn_conv_b, ffn_w_down, norm_mix_pre, norm_mix_post, norm_ffn_pre, norm_ffn_post):
    batch, seq, d = x.shape
    depth = ffn_w_up.shape[0]
    n_mixers = 2
    n_heads = ssd_dt_bias.shape[1]
    assert seq % CHUNK == 0 and n_heads <= LANES and depth % n_mixers == 0
    head_pad = ((0, 0), (0, LANES - n_heads))
    ssd_params = (jnp.pad(ssd_w_in.astype(BF16), ((0, 0),) + head_pad), ssd_conv_w, _rows(ssd_conv_b),
                  _rows(jnp.pad(ssd_dt_bias, head_pad)), _rows(jnp.pad(ssd_a_log, head_pad)),
                  _rows(jnp.repeat(ssd_d, HEAD_DIM, axis=1)), _rows(ssd_norm_w), ssd_w_out.astype(BF16))
    pool_params = (pool_w.astype(BF16), _rows(pool_scale))
    ffn_params = (ffn_w_up.astype(BF16), ffn_conv_w, _rows(ffn_conv_b), ffn_w_down.astype(BF16))
    mix_pre, mix_post = _rows(norm_mix_pre), _rows(norm_mix_post)
    ffn_pre, ffn_post = _rows(norm_ffn_pre), _rows(norm_ffn_post)
    xs = x.reshape(batch * seq, d)
    for i in range(depth):
        j = i // n_mixers
        if i % n_mixers == 0:
            xs = _ssd_layer(xs, batch, i == 0, i, j, mix_pre, *ssd_params, mix_post)
        else:
            xs = _pool_layer(xs, batch, i, j, mix_pre, *pool_params, mix_post)
        xs = _ffn_layer(xs, batch, i, ffn_pre, *ffn_params, ffn_post, natural_out=i + 1 == depth)
    return xs.reshape(batch, seq, d)
```
